```python
import jax, jax.numpy as jnp
from jax import lax
import numpy as np

D_MODEL = 1024
BATCH = 8
SEQ = 2048
DEPTH = 4
DEC_BATCH = 32
DEC_SEQ = 4
PAST_LEN = 16384
PAGE_SIZE = 128

D_MIX = D_MODEL
MLA_V = 64
MLA_HEADS = (D_MIX // 2) // MLA_V
MLA_NOPE = 64
MLA_ROPE = 32
MLA_WIDTH = MLA_HEADS * MLA_V
Q_LORA = 3 * D_MODEL // 8
KV_LORA = D_MODEL // 4
ROPE_BASE = 10000.0
MLA_SCALE = (MLA_NOPE + MLA_ROPE) ** -0.5
Q_BLOCK = 128
CONV_CH = D_MIX // 4
CONV_W = 31
RWKV_WIDTH = D_MIX // 4
RWKV_HEAD = 64
RWKV_HEADS = RWKV_WIDTH // RWKV_HEAD
W_LORA = 32
A_LORA = 32
G_LORA = 64
RWKV_COLS = 3 * RWKV_WIDTH + W_LORA + A_LORA + G_LORA
LNX_EPS = 64e-5
MLA_COLS = Q_LORA + KV_LORA + MLA_ROPE
CONV_COLS = 2 * CONV_CH
IN_COLS = MLA_COLS + CONV_COLS + RWKV_COLS
D_FF = 2816
N_EXPERTS = 8
TOP_K = 2
D_FF_EXPERT = 3584
MOE_BLOCK = 128
N_DENSE = (DEPTH + 1) // 2
N_MOE = DEPTH // 2
PLE_DIM = 256
DEEPNORM_ALPHA = (2 * DEPTH) ** 0.25
DEEPNORM_BETA = (8 * DEPTH) ** -0.25
LN_EPS = 1e-5
RMS_EPS = 1e-6

kernel_name = 'hybrid_mla_conformer_rwkv7_deepnorm_step'


def _layer_norm(x, g, b, eps=LN_EPS):
    xf = x.astype(jnp.float32)
    mu = jnp.mean(xf, axis=-1, keepdims=True)
    var = jnp.mean(jnp.square(xf - mu), axis=-1, keepdims=True)
    return ((xf - mu) * lax.rsqrt(var + eps)).astype(x.dtype) * g + b


def _rms_norm(x, g, eps=RMS_EPS):
    xf = x.astype(jnp.float32)
    return (xf * lax.rsqrt(jnp.mean(jnp.square(xf), axis=-1, keepdims=True) + eps)).astype(x.dtype) * g


def _rope(x, pos):
    half = x.shape[-1] // 2
    inv = ROPE_BASE ** (-jnp.arange(half, dtype=jnp.float32) / half)
    ang = pos.astype(jnp.float32)[:, None] * inv[None, :]
    shape = (pos.shape[0],) + (1,) * (x.ndim - 3) + (half,)
    cos = jnp.cos(ang).reshape(shape)
    sin = jnp.sin(ang).reshape(shape)
    xf = x.astype(jnp.float32)
    x1, x2 = xf[..., :half], xf[..., half:]
    return jnp.concatenate([x1 * cos - x2 * sin, x1 * sin + x2 * cos], axis=-1).astype(x.dtype)


def _latent_scores(q_lat, q_rope, c, kr):
    s = jnp.einsum('bqhc,bkc->bhqk', q_lat, c) + jnp.einsum('bqhr,bkr->bhqk', q_rope, kr)
    return s.astype(jnp.float32) * MLA_SCALE


def _mla_prompt(q_lat, q_rope, ckv, kr):
    B, T, H, C = q_lat.shape
    key_idx = jnp.arange(T)

    def block(i):
        s0 = i * Q_BLOCK
        ql = lax.dynamic_slice_in_dim(q_lat, s0, Q_BLOCK, axis=1)
        qr = lax.dynamic_slice_in_dim(q_rope, s0, Q_BLOCK, axis=1)
        s = _latent_scores(ql, qr, ckv, kr)
        q_idx = s0 + jnp.arange(Q_BLOCK)
        s = jnp.where(key_idx[None, :] <= q_idx[:, None], s, -jnp.inf)
        p = jax.nn.softmax(s, axis=-1).astype(ckv.dtype)
        return jnp.einsum('bhqk,bkc->bqhc', p, ckv)

    o = lax.map(block, jnp.arange(T // Q_BLOCK))
    return jnp.moveaxis(o, 0, 1).reshape(B, T, H, C)


def _mla_sample(q_lat, q_rope, ckv, kr, c_past, kr_past):
    T = q_lat.shape[1]
    n_past = c_past.shape[1]
    s_past = _latent_scores(q_lat, q_rope, c_past, kr_past)
    s_new = _latent_scores(q_lat, q_rope, ckv, kr)
    s_new = jnp.where(jnp.tril(jnp.ones((T, T), dtype=bool)), s_new, -jnp.inf)
    p = jax.nn.softmax(jnp.concatenate([s_past, s_new], axis=-1), axis=-1).astype(ckv.dtype)
    return (jnp.einsum('bhqk,bkc->bqhc', p[..., :n_past], c_past)
            + jnp.einsum('bhqk,bkc->bqhc', p[..., n_past:], ckv))


def _conv_module(cols, buf, lp):
    u = cols[..., :CONV_CH] * jax.nn.sigmoid(cols[..., CONV_CH:])
    up = jnp.concatenate([buf.astype(u.dtype), u], axis=1)
    y = lax.conv_general_dilated(up, lp['conv_w'][:, None, :], window_strides=(1,), padding='VALID',
                                 dimension_numbers=('NWC', 'WIO', 'NWC'),
                                 feature_group_count=CONV_CH) + lp['conv_b']
    y = jax.nn.silu(_layer_norm(y, lp['conv_ln_g'], lp['conv_ln_b']))
    return y, up[:, up.shape[1] - (CONV_W - 1):]


def _rwkv7(cols, prev_row, wkv0, lp):
    B, T, _ = cols.shape
    W = RWKV_WIDTH
    shifted = jnp.concatenate([prev_row[:, None, :].astype(cols.dtype), cols[:, :-1]], axis=1)
    xm = cols + (shifted - cols) * lp['rwkv_mu']
    r = xm[..., :W]
    k = xm[..., W:2 * W]
    v = xm[..., 2 * W:3 * W]
    wl = xm[..., 3 * W:3 * W + W_LORA]
    al = xm[..., 3 * W + W_LORA:3 * W + W_LORA + A_LORA]
    gl = xm[..., 3 * W + W_LORA + A_LORA:]
    w_raw = -jax.nn.softplus(-(lp['rwkv_w0'] + jnp.tanh(wl) @ lp['rwkv_w2']).astype(jnp.float32)) - 0.5
    decay = jnp.exp(-jnp.exp(w_raw)).astype(cols.dtype)
    a = jax.nn.sigmoid(lp['rwkv_a0'] + al @ lp['rwkv_a2'])
    g = jax.nn.sigmoid(gl) @ lp['rwkv_g2']
    hd = (B, T, RWKV_HEADS, RWKV_HEAD)
    r, k, v, decay, a = (t.reshape(hd) for t in (r, k, v, decay, a))
    kk = (k * lp['rwkv_k_k'].reshape(RWKV_HEADS, RWKV_HEAD)).astype(jnp.float32)
    kk = (kk / jnp.maximum(jnp.linalg.norm(kk, axis=-1, keepdims=True), 1e-12)).astype(cols.dtype)
    k = k * (1 + (a - 1) * lp['rwkv_k_a'].reshape(RWKV_HEADS, RWKV_HEAD))

    def step(S, inp):
        r_t, w_t, k_t, v_t, kk_t, a_t = inp
        sa = jnp.einsum('bhij,bhj->bhi', S, -kk_t)
        S = (S * w_t[:, :, None, :] + sa[..., None] * (kk_t * a_t)[:, :, None, :]
             + v_t[..., None] * k_t[:, :, None, :])
        return S, jnp.einsum('bhij,bhj->bhi', S, r_t)

    seq = tuple(jnp.moveaxis(t, 1, 0) for t in (r, decay, k, v, kk, a))
    S_final, y = lax.scan(step, wkv0.astype(cols.dtype), seq)
    y = jnp.moveaxis(y, 0, 1).astype(jnp.float32)
    mu = jnp.mean(y, axis=-1, keepdims=True)
    var = jnp.mean(jnp.square(y - mu), axis=-1, keepdims=True)
    yn = ((y - mu) * lax.rsqrt(var + LNX_EPS)).astype(cols.dtype).reshape(B, T, W)
    yn = yn * lp['rwkv_lnx_g'] + lp['rwkv_lnx_b']
    bonus = (jnp.sum(r * k * lp['rwkv_r_k'], axis=-1, keepdims=True) * v).reshape(B, T, W)
    return (yn + bonus) * g, cols[:, -1], S_final


def _mixers(x, pos, lp, conv_buf, shift_buf, wkv0, c_past, kr_past):
    B, T, _ = x.shape
    cols = x @ lp['w_in']
    o1 = Q_LORA
    o2 = o1 + KV_LORA
    o3 = o2 + MLA_ROPE
    o4 = o3 + CONV_COLS
    cq = _rms_norm(cols[..., :o1], lp['q_norm_g'])
    q = (cq @ lp['w_uq']).reshape(B, T, MLA_HEADS, MLA_NOPE + MLA_ROPE)
    q_rope = _rope(q[..., MLA_NOPE:], pos)
    q_lat = jnp.einsum('bthn,chn->bthc', q[..., :MLA_NOPE], lp['w_uk'])
    ckv = _rms_norm(cols[..., o1:o2], lp['kv_norm_g'])
    kr = _rope(cols[..., o2:o3], pos)
    if c_past is None:
        o_lat = _mla_prompt(q_lat, q_rope, ckv, kr)
    else:
        o_lat = _mla_sample(q_lat, q_rope, ckv, kr, c_past, kr_past)
    attn = jnp.einsum('bthc,chv->bthv', o_lat, lp['w_uv']).reshape(B, T, MLA_WIDTH)
    attn = _rms_norm(attn, lp['mla_out_g'])
    conv, conv_new = _conv_module(cols[..., o3:o4], conv_buf, lp)
    rwkv, shift_new, wkv_new = _rwkv7(cols[..., o4:], shift_buf, wkv0, lp)
    h = jnp.concatenate([attn, conv, rwkv], axis=-1) @ lp['w_out']
    return h, (ckv, kr, conv_new, shift_new, wkv_new)


def _swiglu(x, wg, wu, wd):
    return (jax.nn.silu(x @ wg) * (x @ wu)) @ wd


def _moe_swiglu(x, w_router, w_gate, w_up, w_down):
    B, T, D = x.shape
    n_tok = B * T
    xt = x.reshape(n_tok, D)
    logits = (xt @ w_router).astype(jnp.float32)
    top_val, top_idx = lax.top_k(logits, TOP_K)
    gates = jax.nn.softmax(top_val, axis=-1).astype(x.dtype)
    n_assign = n_tok * TOP_K
    flat_e = top_idx.reshape(-1).astype(jnp.int32)
    flat_tok = jnp.repeat(jnp.arange(n_tok, dtype=jnp.int32), TOP_K)
    flat_g = gates.reshape(-1)
    order = jnp.argsort(flat_e)
    se, stok, sg = flat_e[order], flat_tok[order], flat_g[order]
    counts = jnp.bincount(flat_e, length=N_EXPERTS).astype(jnp.int32)
    start = jnp.cumsum(counts) - counts
    padded = (counts + MOE_BLOCK - 1) // MOE_BLOCK * MOE_BLOCK
    pad_end = jnp.cumsum(padded)
    pad_start = pad_end - padded
    dest = pad_start[se] + (jnp.arange(n_assign, dtype=jnp.int32) - start[se])
    n_blocks = -(-n_assign // MOE_BLOCK) + N_EXPERTS
    n_rows = n_blocks * MOE_BLOCK
    row_tok = jnp.full((n_rows,), n_tok, jnp.int32).at[dest].set(stok)
    block_e = jnp.minimum(jnp.searchsorted(pad_end, jnp.arange(n_blocks, dtype=jnp.int32) * MOE_BLOCK,
                                           side='right'), N_EXPERTS - 1)
    x_pad = jnp.concatenate([xt, jnp.zeros((1, D), xt.dtype)], axis=0)

    def run_block(args):
        toks, e = args
        xb = x_pad[toks]
        return (jax.nn.silu(xb @ w_gate[e]) * (xb @ w_up[e])) @ w_down[e]

    out_rows = lax.map(run_block, (row_tok.reshape(n_blocks, MOE_BLOCK), block_e)).reshape(n_rows, D)
    y = jax.ops.segment_sum(out_rows[dest] * sg[:, None], stok, num_segments=n_tok)
    return y.reshape(B, T, D)


def setup_inputs(seed: int = 0) -> dict:
    key = jax.random.key(seed)
    ks = iter(jax.random.split(key, 64))

    def nrm(shape, scale):
        return jax.random.normal(next(ks), shape, jnp.float32) * scale

    def gain(shape):
        return 1.0 + nrm(shape, 0.05)

    n_pages = PAST_LEN // PAGE_SIZE
    n_pool = (DEC_BATCH * n_pages * 5) // 4
    perm = jax.random.permutation(next(ks), n_pool)
    page_table = perm[:DEC_BATCH * n_pages].reshape(DEC_BATCH, n_pages).astype(jnp.int32)
    L = DEPTH
    return {
        'x_prompt': nrm((BATCH, SEQ, D_MODEL), 1.0),
        'x_sample': nrm((DEC_BATCH, DEC_SEQ, D_MODEL), 1.0),
        'cache_ckv': nrm((L, n_pool, PAGE_SIZE, KV_LORA), 1.0),
        'cache_kr': nrm((L, n_pool, PAGE_SIZE, MLA_ROPE), 1.0),
        'state_conv': nrm((L, DEC_BATCH, CONV_W - 1, CONV_CH), 0.5),
        'state_shift': nrm((L, DEC_BATCH, RWKV_COLS), 1.0),
        'state_wkv': nrm((L, DEC_BATCH, RWKV_HEADS, RWKV_HEAD, RWKV_HEAD), 0.3),
        'page_table': page_table,
        'p_prompt': nrm((L, BATCH, SEQ, PLE_DIM), 1.0),
        'p_sample': nrm((L, DEC_BATCH, DEC_SEQ, PLE_DIM), 1.0),
        'w_in': nrm((L, D_MODEL, IN_COLS), D_MODEL ** -0.5),
        'q_norm_g': gain((L, Q_LORA)),
        'w_uq': nrm((L, Q_LORA, MLA_HEADS * (MLA_NOPE + MLA_ROPE)), Q_LORA ** -0.5),
        'kv_norm_g': gain((L, KV_LORA)),
        'w_uk': nrm((L, KV_LORA, MLA_HEADS, MLA_NOPE), KV_LORA ** -0.5),
        'w_uv': nrm((L, KV_LORA, MLA_HEADS, MLA_V), KV_LORA ** -0.5),
        'mla_out_g': gain((L, MLA_WIDTH)),
        'conv_w': nrm((L, CONV_W, CONV_CH), CONV_W ** -0.5),
        'conv_b': nrm((L, CONV_CH), 0.02),
        'conv_ln_g': gain((L, CONV_CH)),
        'conv_ln_b': nrm((L, CONV_CH), 0.02),
        'rwkv_mu': jax.random.uniform(next(ks), (L, RWKV_COLS), jnp.float32),
        'rwkv_w0': jax.random.uniform(next(ks), (L, RWKV_WIDTH), jnp.float32, -4.0, 1.0),
        'rwkv_w2': nrm((L, W_LORA, RWKV_WIDTH), 0.1 * W_LORA ** -0.5),
        'rwkv_a0': nrm((L, RWKV_WIDTH), 0.1),
        'rwkv_a2': nrm((L, A_LORA, RWKV_WIDTH), 0.1 * A_LORA ** -0.5),
        'rwkv_g2': nrm((L, G_LORA, RWKV_WIDTH), G_LORA ** -0.5),
        'rwkv_k_k': 0.85 + nrm((L, RWKV_WIDTH), 0.05),
        'rwkv_k_a': gain((L, RWKV_WIDTH)),
        'rwkv_r_k': nrm((L, RWKV_HEADS, RWKV_HEAD), 0.1),
        'rwkv_lnx_g': gain((L, RWKV_WIDTH)),
        'rwkv_lnx_b': nrm((L, RWKV_WIDTH), 0.02),
        'w_out': nrm((L, D_MIX, D_MODEL), D_MIX ** -0.5 * DEEPNORM_BETA),
        'ln1_g': gain((L, D_MODEL)),
        'ln1_b': nrm((L, D_MODEL), 0.02),
        'ffn_w_gate': nrm((N_DENSE, D_MODEL, D_FF), D_MODEL ** -0.5),
        'ffn_w_up': nrm((N_DENSE, D_MODEL, D_FF), D_MODEL ** -0.5),
        'ffn_w_down': nrm((N_DENSE, D_FF, D_MODEL), D_FF ** -0.5 * DEEPNORM_BETA),
        'moe_router': nrm((N_MOE, D_MODEL, N_EXPERTS), D_MODEL ** -0.5),
        'moe_w_gate': nrm((N_MOE, N_EXPERTS, D_MODEL, D_FF_EXPERT), D_MODEL ** -0.5),
        'moe_w_up': nrm((N_MOE, N_EXPERTS, D_MODEL, D_FF_EXPERT), D_MODEL ** -0.5),
        'moe_w_down': nrm((N_MOE, N_EXPERTS, D_FF_EXPERT, D_MODEL), D_FF_EXPERT ** -0.5 * DEEPNORM_BETA),
        'ln2_g': gain((L, D_MODEL)),
        'ln2_b': nrm((L, D_MODEL), 0.02),
        'ple_w': nrm((L, PLE_DIM, D_MODEL), 0.5 * PLE_DIM ** -0.5),
        'ple_gate_w': nrm((L, D_MODEL, D_MODEL), D_MODEL ** -0.5),
    }


def reference(x_prompt, x_sample, cache_ckv, cache_kr, state_conv, state_shift, state_wkv, page_table,
              p_prompt, p_sample, w_in, q_norm_g, w_uq, kv_norm_g, w_uk, w_uv, mla_out_g,
              conv_w, conv_b, conv_ln_g, conv_ln_b, rwkv_mu, rwkv_w0, rwkv_w2, rwkv_a0, rwkv_a2,
              rwkv_g2, rwkv_k_k, rwkv_k_a, rwkv_r_k, rwkv_lnx_g, rwkv_lnx_b, w_out, ln1_g, ln1_b,
              ffn_w_gate, ffn_w_up, ffn_w_down, moe_router, moe_w_gate, moe_w_up, moe_w_down,
              ln2_g, ln2_b, ple_w, ple_gate_w):
    layers = [dict(w_in=w_in[l], q_norm_g=q_norm_g[l], w_uq=w_uq[l], kv_norm_g=kv_norm_g[l],
                   w_uk=w_uk[l], w_uv=w_uv[l], mla_out_g=mla_out_g[l], conv_w=conv_w[l],
                   conv_b=conv_b[l], conv_ln_g=conv_ln_g[l], conv_ln_b=conv_ln_b[l],
                   rwkv_mu=rwkv_mu[l], rwkv_w0=rwkv_w0[l], rwkv_w2=rwkv_w2[l], rwkv_a0=rwkv_a0[l],
                   rwkv_a2=rwkv_a2[l], rwkv_g2=rwkv_g2[l], rwkv_k_k=rwkv_k_k[l], rwkv_k_a=rwkv_k_a[l],
                   rwkv_r_k=rwkv_r_k[l], rwkv_lnx_g=rwkv_lnx_g[l], rwkv_lnx_b=rwkv_lnx_b[l],
                   w_out=w_out[l]) for l in range(DEPTH)]
    n_seq, n_pages = page_table.shape
    past_len = n_pages * PAGE_SIZE

    def run_group(x, p_emb, pos, with_past):
        B = x.shape[0]
        news = []
        for l in range(DEPTH):
            if with_past:
                c_past = cache_ckv[l][page_table].reshape(n_seq, past_len, KV_LORA)
                kr_past = cache_kr[l][page_table].reshape(n_seq, past_len, MLA_ROPE)
                conv_buf, shift_buf, wkv0 = state_conv[l], state_shift[l], state_wkv[l]
            else:
                c_past = kr_past = None
                conv_buf = jnp.zeros((B, CONV_W - 1, CONV_CH), x.dtype)
                shift_buf = jnp.zeros((B, RWKV_COLS), x.dtype)
                wkv0 = jnp.zeros((B, RWKV_HEADS, RWKV_HEAD, RWKV_HEAD), x.dtype)
            h, st = _mixers(x, pos, layers[l], conv_buf, shift_buf, wkv0, c_past, kr_past)
            x = _layer_norm(DEEPNORM_ALPHA * x + h, ln1_g[l], ln1_b[l])
            j = l // 2
            if l % 2 == 0:
                f = _swiglu(x, ffn_w_gate[j], ffn_w_up[j], ffn_w_down[j])
            else:
                f = _moe_swiglu(x, moe_router[j], moe_w_gate[j], moe_w_up[j], moe_w_down[j])
            x = _layer_norm(DEEPNORM_ALPHA * x + f, ln2_g[l], ln2_b[l])
            x = x + jax.nn.sigmoid(x @ ple_gate_w[l]) * (p_emb[l] @ ple_w[l])
            news.append(st)
        ckv, kr, conv, shift, wkv = (jnp.stack([st[i] for st in news]) for i in range(5))
        return x, ckv, kr, conv, shift, wkv

    pos_prompt = jnp.arange(x_prompt.shape[1], dtype=jnp.int32)
    pos_sample = past_len + jnp.arange(x_sample.shape[1], dtype=jnp.int32)
    y_prompt, ckv_p, kr_p, conv_p, shift_p, wkv_p = run_group(x_prompt, p_prompt, pos_prompt, False)
    y_sample, ckv_s, kr_s, conv_s, shift_s, wkv_s = run_group(x_sample, p_sample, pos_sample, True)
    return (y_prompt, y_sample, ckv_p, kr_p, conv_p, shift_p, wkv_p, ckv_s, kr_s, conv_s, shift_s, wkv_s)
```

```python
import functools

import jax
import jax.numpy as jnp
from jax import lax
from jax.experimental import pallas as pl
from jax.experimental.pallas import tpu as pltpu

F32 = jnp.float32
BF16 = jnp.bfloat16

D_MODEL = 1024
MLA_HEADS = 8
MLA_NOPE = 64
MLA_ROPE = 32
MLA_V = 64
MLA_WIDTH = MLA_HEADS * MLA_V
Q_LORA = 384
KV_LORA = 256
ROPE_BASE = 10000.0
MLA_SCALE = (MLA_NOPE + MLA_ROPE) ** -0.5
CONV_CH = 256
CONV_W = 31
RWKV_WIDTH = 256
RWKV_HEAD = 64
RWKV_HEADS = 4
W_LORA = 32
A_LORA = 32
G_LORA = 64
RWKV_COLS = 3 * RWKV_WIDTH + W_LORA + A_LORA + G_LORA
LNX_EPS = 64e-5
N_EXPERTS = 8
TOP_K = 2
PAGE_SIZE = 128
LN_EPS = 1e-5
RMS_EPS = 1e-6

LANES = 128
SUBLANES = 8
VMEM_LIMIT_BYTES = 56 * 1024 * 1024
QK_WIDTH = 2 * KV_LORA

ROW_TILE = 512
ATTN_TQ = 128
ATTN_TK = 512
PAGES_PER_STEP = 16
NEW_ROWS_PAD = 16
SCAN_STEPS = 64
MOE_ROWS = 512
GATHER_ROWS = 256
ROUTER_ROWS = 16


def _tile(n, pref, mult=SUBLANES):
    if n <= pref:
        return n
    for t in range(pref, 0, -1):
        if n % t == 0 and t % mult == 0:
            return t
    return n


def _params(*sem):
    return pltpu.CompilerParams(dimension_semantics=sem, vmem_limit_bytes=VMEM_LIMIT_BYTES)


def _dot(a, b):
    return jnp.dot(a, b, preferred_element_type=F32)


def _dot_nt(a, b):
    return lax.dot_general(a, b, (((1,), (1,)), ((), ())), preferred_element_type=F32)


def _split3(x):
    hi = x.astype(BF16)
    r1 = x - hi.astype(F32)
    mid = r1.astype(BF16)
    lo = (r1 - mid.astype(F32)).astype(BF16)
    return hi, mid, lo


def _segsum(x, ones_bd):
    hi, mid, lo = _split3(x)
    return _dot(hi, ones_bd) + _dot(mid, ones_bd) + _dot(lo, ones_bd)


def _layer_norm_rows(z, g, b):
    mu = jnp.mean(z, axis=-1, keepdims=True)
    d = z - mu
    var = jnp.mean(d * d, axis=-1, keepdims=True)
    return d * lax.rsqrt(var + LN_EPS) * g + b


def _rms_norm_rows(z, g):
    return z * lax.rsqrt(jnp.mean(z * z, axis=-1, keepdims=True) + RMS_EPS) * g


def _sigmoid(z):
    return 1.0 / (1.0 + jnp.exp(-z))


def _full(shape):
    nd = len(shape)
    return pl.BlockSpec(shape, lambda *_: (0,) * nd)


def _proj_kernel(x_ref, cos_ref, sin_ref, wq_ref, wkv_ref, wkr_ref, wkrr_ref, wcv_ref, wrw_ref,
                 gq_ref, gkv_ref, wuqn_ref, wuqr_ref, wuqrr_ref, wukp_ref,
                 q_ref, kvb_ref, ckv_ref, kr_ref, u_ref, rc_ref):
    xb = x_ref[...].astype(BF16)
    cos = cos_ref[...]
    sin = sin_ref[...]
    cq = _rms_norm_rows(_dot(xb, wq_ref[...]), gq_ref[...]).astype(BF16)
    qn = _dot(cq, wuqn_ref[...]).astype(BF16)
    qr = (_dot(cq, wuqr_ref[...]) * cos + _dot(cq, wuqrr_ref[...]) * sin) * MLA_SCALE
    lane = lax.broadcasted_iota(jnp.int32, qr.shape, 1)
    for h in range(MLA_HEADS):
        slab = qn[:, (h // 2) * LANES:(h // 2 + 1) * LANES]
        q_ref[h, :, 0:KV_LORA] = (_dot(slab, wukp_ref[h]) * MLA_SCALE).astype(BF16)
        own = (lane >= h * MLA_ROPE) & (lane < (h + 1) * MLA_ROPE)
        q_ref[h, :, KV_LORA:QK_WIDTH] = jnp.where(own, qr, 0.0).astype(BF16)
    ckv = _rms_norm_rows(_dot(xb, wkv_ref[...]), gkv_ref[...])
    ckv_ref[...] = ckv
    kvb_ref[:, 0:KV_LORA] = ckv.astype(BF16)
    kr8 = _dot(xb, wkr_ref[...]) * cos + _dot(xb, wkrr_ref[...]) * sin
    kvb_ref[:, KV_LORA:QK_WIDTH] = kr8.astype(BF16)
    kr_ref[...] = kr8[:, 0:MLA_ROPE]
    cv = _dot(xb, wcv_ref[...])
    u_ref[...] = cv[:, 0:CONV_CH] * _sigmoid(cv[:, CONV_CH:2 * CONV_CH])
    rc_ref[...] = _dot(xb, wrw_ref[...])


def _proj(x, cos_t, sin_t, w, T):
    n = x.shape[0]
    tm = _tile(n, ROW_TILE)
    if T % tm == 0:
        nt = T // tm
        tab_map = lambda i: (i % nt, 0)
    else:
        assert tm % T == 0
        cos_t = jnp.tile(cos_t, (tm // T, 1))
        sin_t = jnp.tile(sin_t, (tm // T, 1))
        tab_map = lambda i: (0, 0)
    row = lambda c: pl.BlockSpec((tm, c), lambda i: (i, 0))
    wts = [w['wq'], w['wkv'], w['wkr8'], w['wkrr8'], w['wcv'], w['wrw'], w['gq'], w['gkv'],
           w['wuqn'], w['wuqr'], w['wuqrr'], w['wukp']]
    return pl.pallas_call(
        _proj_kernel,
        grid=(n // tm,),
        in_specs=[row(D_MODEL), pl.BlockSpec((tm, KV_LORA), tab_map), pl.BlockSpec((tm, KV_LORA), tab_map)]
                 + [_full(a.shape) for a in wts],
        out_specs=[pl.BlockSpec((MLA_HEADS, tm, QK_WIDTH), lambda i: (0, i, 0)),
                   row(QK_WIDTH), row(KV_LORA), row(MLA_ROPE), row(CONV_CH), row(RWKV_COLS)],
        out_shape=[jax.ShapeDtypeStruct((MLA_HEADS, n, QK_WIDTH), BF16),
                   jax.ShapeDtypeStruct((n, QK_WIDTH), BF16),
                   jax.ShapeDtypeStruct((n, KV_LORA), F32),
                   jax.ShapeDtypeStruct((n, MLA_ROPE), F32),
                   jax.ShapeDtypeStruct((n, CONV_CH), F32),
                   jax.ShapeDtypeStruct((n, RWKV_COLS), F32)],
        compiler_params=_params("parallel"),
        name="proj",
    )(x, cos_t, sin_t, *wts)


def _attn_prompt_kernel(q_ref, kv_ref, o_ref, m_ref, l_ref, acc_ref, *, tq, tk):
    qi = pl.program_id(1)
    rows = MLA_HEADS * tq
    q = q_ref[...].reshape(rows, QK_WIDTH)
    m_ref[...] = jnp.full(m_ref.shape, -jnp.inf, F32)
    l_ref[...] = jnp.zeros(l_ref.shape, F32)
    acc_ref[...] = jnp.zeros(acc_ref.shape, F32)
    q_pos = qi * tq + (lax.broadcasted_iota(jnp.int32, (rows, 1), 0) & (tq - 1))
    n_kb = ((qi + 1) * tq + tk - 1) // tk

    def body(kb, carry):
        k0 = pl.multiple_of(kb * tk, tk)
        kv = kv_ref[pl.ds(k0, tk), :]
        s = _dot_nt(q, kv)
        k_pos = k0 + lax.broadcasted_iota(jnp.int32, (1, tk), 1)
        s = jnp.where(k_pos <= q_pos, s, -jnp.inf)
        m_prev = m_ref[...]
        m_new = jnp.maximum(m_prev, jnp.max(s, axis=-1, keepdims=True))
        p = jnp.exp(s - m_new)
        alpha = jnp.exp(m_prev - m_new)
        l_ref[...] = alpha * l_ref[...] + jnp.sum(p, axis=-1, keepdims=True)
        acc_ref[...] = alpha * acc_ref[...] + _dot(p.astype(BF16), kv[:, 0:KV_LORA])
        m_ref[...] = m_new
        return carry

    lax.fori_loop(0, n_kb, body, 0)
    o = acc_ref[...] / l_ref[...]
    o_ref[...] = o.reshape(MLA_HEADS, tq, KV_LORA).astype(BF16)


def _attn_prompt(q, kvb, B, T):
    n = B * T
    tq = _tile(T, ATTN_TQ, 16)
    tk = _tile(T, ATTN_TK, 16)
    assert tq & (tq - 1) == 0
    nq = T // tq
    rows = MLA_HEADS * tq
    return pl.pallas_call(
        functools.partial(_attn_prompt_kernel, tq=tq, tk=tk),
        grid=(B, nq),
        in_specs=[pl.BlockSpec((MLA_HEADS, tq, QK_WIDTH), lambda b, i: (0, b * nq + i, 0)),
                  pl.BlockSpec((T, QK_WIDTH), lambda b, i: (b, 0))],
        out_specs=pl.BlockSpec((MLA_HEADS, tq, KV_LORA), lambda b, i: (0, b * nq + i, 0)),
        out_shape=jax.ShapeDtypeStruct((MLA_HEADS, n, KV_LORA), BF16),
        scratch_shapes=[pltpu.VMEM((rows, 1), F32), pltpu.VMEM((rows, 1), F32),
                        pltpu.VMEM((rows, KV_LORA), F32)],
        compiler_params=_params("parallel", "parallel"),
        name="attn_prompt",
    )(q, kvb)


def _attn_sample_kernel(pt_ref, ql_ref, qr_ref, *refs, n_pg, n_new):
    del pt_ref
    c_pages = refs[:n_pg]
    r_pages = refs[n_pg:2 * n_pg]
    cn_ref, rn_ref, o_ref, m_ref, l_ref, acc_ref = refs[2 * n_pg:]
    g = pl.program_id(1)

    @pl.when(g == 0)
    def _():
        m_ref[...] = jnp.full(m_ref.shape, -jnp.inf, F32)
        l_ref[...] = jnp.zeros(l_ref.shape, F32)
        acc_ref[...] = jnp.zeros(acc_ref.shape, F32)

    ql = ql_ref[0]
    qr = qr_ref[0]

    def update(s, vals):
        m_prev = m_ref[...]
        m_new = jnp.maximum(m_prev, jnp.max(s, axis=-1, keepdims=True))
        p = jnp.exp(s - m_new)
        alpha = jnp.exp(m_prev - m_new)
        l_ref[...] = alpha * l_ref[...] + jnp.sum(p, axis=-1, keepdims=True)
        acc_ref[...] = alpha * acc_ref[...] + _dot(p.astype(BF16), vals)
        m_ref[...] = m_new

    cs = [c_pages[k][0, 0].astype(BF16) for k in range(n_pg)]
    s = jnp.concatenate([_dot_nt(ql, cs[k]) + _dot_nt(qr, r_pages[k][0, 0].astype(BF16))
                         for k in range(n_pg)], axis=-1)
    update(s, jnp.concatenate(cs, axis=0))

    @pl.when(g == pl.num_programs(1) - 1)
    def _():
        cn = cn_ref[0].astype(BF16)
        s_new = _dot_nt(ql, cn) + _dot_nt(qr, rn_ref[0].astype(BF16))
        t_q = lax.broadcasted_iota(jnp.int32, s_new.shape, 0) & (n_new - 1)
        t_k = lax.broadcasted_iota(jnp.int32, s_new.shape, 1)
        update(jnp.where(t_k <= t_q, s_new, -jnp.inf), cn)
        o_ref[0] = acc_ref[...] / l_ref[...]


def _attn_sample(ql, qr, ckv_new, kr_new, cache_ckv, cache_kr, page_table, layer):
    B, rows, _ = ql.shape
    n_new = ckv_new.shape[1]
    assert n_new & (n_new - 1) == 0
    n_pad = max(NEW_ROWS_PAD, n_new)
    ckv_new = jnp.pad(ckv_new, ((0, 0), (0, n_pad - n_new), (0, 0)))
    kr_new = jnp.pad(kr_new, ((0, 0), (0, n_pad - n_new), (0, 0)))
    n_pages = page_table.shape[1]
    n_pg = _tile(n_pages, PAGES_PER_STEP, 1)
    n_g = n_pages // n_pg

    def page_spec(width, k):
        return pl.BlockSpec((1, 1, PAGE_SIZE, width),
                            lambda b, g, pt: (layer, pt[b * n_pages + g * n_pg + k], 0, 0))

    grid_spec = pltpu.PrefetchScalarGridSpec(
        num_scalar_prefetch=1,
        grid=(B, n_g),
        in_specs=[pl.BlockSpec((1, rows, KV_LORA), lambda b, g, pt: (b, 0, 0)),
                  pl.BlockSpec((1, rows, MLA_ROPE), lambda b, g, pt: (b, 0, 0))]
                 + [page_spec(KV_LORA, k) for k in range(n_pg)]
                 + [page_spec(MLA_ROPE, k) for k in range(n_pg)]
                 + [pl.BlockSpec((1, n_pad, KV_LORA), lambda b, g, pt: (b, 0, 0)),
                    pl.BlockSpec((1, n_pad, MLA_ROPE), lambda b, g, pt: (b, 0, 0))],
        out_specs=pl.BlockSpec((1, rows, KV_LORA), lambda b, g, pt: (b, 0, 0)),
        scratch_shapes=[pltpu.VMEM((rows, 1), F32), pltpu.VMEM((rows, 1), F32),
                        pltpu.VMEM((rows, KV_LORA), F32)],
    )
    return pl.pallas_call(
        functools.partial(_attn_sample_kernel, n_pg=n_pg, n_new=n_new),
        grid_spec=grid_spec,
        out_shape=jax.ShapeDtypeStruct((B, rows, KV_LORA), F32),
        compiler_params=_params("parallel", "arbitrary"),
        name="attn_sample",
    )(page_table.reshape(-1), ql, qr, *([cache_ckv] * n_pg), *([cache_kr] * n_pg), ckv_new, kr_new)


def _uvnorm_kernel(o_ref, wuvp_ref, g_ref, out_ref):
    parts = []
    for p in range(MLA_HEADS // 2):
        pair = jnp.concatenate([o_ref[2 * p], o_ref[2 * p + 1]], axis=-1)
        parts.append(_dot(pair, wuvp_ref[p]))
    a = jnp.concatenate(parts, axis=-1)
    out_ref[...] = _rms_norm_rows(a, g_ref[...]).astype(BF16)


def _uvnorm(o, wuvp, g):
    n = o.shape[1]
    tm = _tile(n, ROW_TILE)
    return pl.pallas_call(
        _uvnorm_kernel,
        grid=(n // tm,),
        in_specs=[pl.BlockSpec((MLA_HEADS, tm, KV_LORA), lambda i: (0, i, 0)), _full(wuvp.shape), _full(g.shape)],
        out_specs=pl.BlockSpec((tm, MLA_WIDTH), lambda i: (i, 0)),
        out_shape=jax.ShapeDtypeStruct((n, MLA_WIDTH), BF16),
        compiler_params=_params("parallel"),
        name="uv_norm",
    )(o, wuvp, g)


CONV_PAD = 32


def _conv_kernel(u_ref, buf_ref, w_ref, b_ref, g_ref, bb_ref, y_ref, new_ref, up_ref, *, T, tt):
    lead = CONV_PAD - (CONV_W - 1)
    up_ref[lead:CONV_PAD, :] = buf_ref[0]
    up_ref[CONV_PAD:CONV_PAD + T, :] = u_ref[0]
    for t0 in range(0, T, tt):
        acc = jnp.zeros((tt, CONV_CH), F32) + b_ref[...]
        for k in range(CONV_W):
            acc = acc + up_ref[lead + t0 + k:lead + t0 + k + tt, :] * w_ref[k:k + 1, :]
        z = _layer_norm_rows(acc, g_ref[...], bb_ref[...])
        y_ref[0, t0:t0 + tt, :] = z * _sigmoid(z)
    new_ref[0] = up_ref[lead + T:CONV_PAD + T, :]


def _conv(u, buf, w, b, g, bb):
    B, T, _ = u.shape
    tt = _tile(T, 256)
    return pl.pallas_call(
        functools.partial(_conv_kernel, T=T, tt=tt),
        grid=(B,),
        in_specs=[pl.BlockSpec((1, T, CONV_CH), lambda i: (i, 0, 0)),
                  pl.BlockSpec((1, CONV_W - 1, CONV_CH), lambda i: (i, 0, 0)),
                  _full(w.shape), _full(b.shape), _full(g.shape), _full(bb.shape)],
        out_specs=[pl.BlockSpec((1, T, CONV_CH), lambda i: (i, 0, 0)),
                   pl.BlockSpec((1, CONV_W - 1, CONV_CH), lambda i: (i, 0, 0))],
        out_shape=[jax.ShapeDtypeStruct((B, T, CONV_CH), F32),
                   jax.ShapeDtypeStruct((B, CONV_W - 1, CONV_CH), F32)],
        scratch_shapes=[pltpu.VMEM((CONV_PAD + T, CONV_CH), F32)],
        compiler_params=_params("parallel"),
        name="conv_module",
    )(u, buf, w, b, g, bb)


def _rwkv_prep_kernel(c_ref, s_ref, mu_ref, w0_ref, a0_ref, kk_ref, ka_ref, rk_ref,
                      w2_ref, a2_ref, g2_ref, ones_ref,
                      r_out, w_out, nkk_out, kka_out, k_out, v_out, g_out, bonus_out):
    cols = c_ref[...]
    xm = cols + (s_ref[...] - cols) * mu_ref[...]
    W = RWKV_WIDTH
    r = xm[:, 0:W]
    k = xm[:, W:2 * W]
    v = xm[:, 2 * W:3 * W]
    tail = xm[:, 3 * W:3 * W + LANES]
    ones_bd = ones_ref[...]
    z = w0_ref[...] + _dot(jnp.tanh(tail).astype(BF16), w2_ref[...])
    softplus = jnp.maximum(-z, 0.0) + jnp.log(1.0 + jnp.exp(-jnp.abs(z)))
    decay = jnp.exp(-jnp.exp(-softplus - 0.5))
    a = _sigmoid(a0_ref[...] + _dot(tail.astype(BF16), a2_ref[...]))
    g = _dot(_sigmoid(tail).astype(BF16), g2_ref[...])
    kk = k * kk_ref[...]
    norm = jnp.sqrt(_segsum(kk * kk, ones_bd))
    kk = kk / jnp.maximum(norm, 1e-12)
    kp = k * (1.0 + (a - 1.0) * ka_ref[...])
    r_out[...] = r
    w_out[...] = decay
    nkk_out[...] = -kk
    kka_out[...] = kk * a
    k_out[...] = kp
    v_out[...] = v
    g_out[...] = g
    bonus_out[...] = _segsum(r * kp * rk_ref[...], ones_bd) * v


def _rwkv_prep(cols, shifted, w):
    n = cols.shape[0]
    tm = _tile(n, ROW_TILE)
    row = lambda c: pl.BlockSpec((tm, c), lambda i: (i, 0))
    wts = [w['rwkv_mu'], w['rwkv_w0'], w['rwkv_a0'], w['rwkv_k_k'], w['rwkv_k_a'], w['rwkv_r_k'],
           w['w2p'], w['a2p'], w['g2p'], w['ones_bd']]
    return pl.pallas_call(
        _rwkv_prep_kernel,
        grid=(n // tm,),
        in_specs=[row(RWKV_COLS), row(RWKV_COLS)] + [_full(a.shape) for a in wts],
        out_specs=[row(RWKV_WIDTH)] * 8,
        out_shape=[jax.ShapeDtypeStruct((n, RWKV_WIDTH), F32)] * 8,
        compiler_params=_params("parallel"),
        name="rwkv_prep",
    )(cols, shifted, *wts)


def _sublane_allsum(x):
    x = x + pltpu.roll(x, 4, 0)
    x = x + pltpu.roll(x, 2, 0)
    return x + pltpu.roll(x, 1, 0)


def _scan_kernel(w_ref, nkk_ref, kka_ref, k_ref, r_ref, v_ref, s0_ref, y_ref, s_ref, *, n_steps, n_ih, unroll):
    @pl.when(pl.program_id(0) == 0)
    def _():
        s_ref[...] = s0_ref[...]

    nv = RWKV_HEAD // SUBLANES
    shape3 = (nv, SUBLANES, LANES)

    def step(t, carry):
        def group(gi, c2):
            for u in range(unroll):
                ih = gi * unroll + u
                s = s_ref[ih].reshape(shape3)
                sa = _sublane_allsum(jnp.sum(s * nkk_ref[t].reshape(shape3), axis=0))
                v_row = v_ref[t, pl.ds(ih, 1), :]
                s = (s * w_ref[t].reshape(shape3) + kka_ref[t].reshape(shape3) * sa[None]
                     + k_ref[t].reshape(shape3) * v_row[None])
                s_ref[ih] = s.reshape(RWKV_HEAD, LANES)
                y = _sublane_allsum(jnp.sum(s * r_ref[t].reshape(shape3), axis=0))
                y_ref[t, pl.ds(ih, 1), :] = y[0:1, :]
            return c2

        lax.fori_loop(0, n_ih // unroll, group, 0)
        return carry

    lax.fori_loop(0, n_steps, step, 0)


def _scan(ops, v_rows, s0):
    T = v_rows.shape[0]
    n_ih = s0.shape[0]
    tc = _tile(T, SCAN_STEPS, 1)
    unroll = _tile(n_ih, 16, 1)
    key_spec = pl.BlockSpec((tc, RWKV_HEAD, LANES), lambda i: (i, 0, 0))
    val_spec = pl.BlockSpec((tc, n_ih, LANES), lambda i: (i, 0, 0))
    return pl.pallas_call(
        functools.partial(_scan_kernel, n_steps=tc, n_ih=n_ih, unroll=unroll),
        grid=(T // tc,),
        in_specs=[key_spec] * 5 + [val_spec, _full(s0.shape)],
        out_specs=[val_spec, _full(s0.shape)],
        out_shape=[jax.ShapeDtypeStruct((T, n_ih, LANES), F32), jax.ShapeDtypeStruct(s0.shape, F32)],
        compiler_params=_params("arbitrary"),
        name="rwkv_scan",
    )(*ops, v_rows, s0)


def _to_key_tiles(x, B, T, i_lo):
    x = x.reshape(B, T, RWKV_HEADS, RWKV_HEAD).transpose(1, 3, 0, 2).reshape(T, RWKV_HEAD, B * RWKV_HEADS)
    return jnp.tile(x, (1, 1, i_lo))


def _to_val_rows(x, B, T, i_lo):
    n_ih = RWKV_HEAD // i_lo
    x = x.reshape(B, T, RWKV_HEADS, n_ih, i_lo).transpose(1, 3, 4, 0, 2)
    return x.reshape(T, n_ih, i_lo * B * RWKV_HEADS)


def _from_val_rows(y, B, T, i_lo):
    n_ih = RWKV_HEAD // i_lo
    y = y.reshape(T, n_ih, i_lo, B, RWKV_HEADS).transpose(3, 0, 4, 1, 2)
    return y.reshape(B * T, RWKV_WIDTH)


def _state_to_tiles(s, i_lo):
    B = s.shape[0]
    n_ih = RWKV_HEAD // i_lo
    s = s.reshape(B, RWKV_HEADS, n_ih, i_lo, RWKV_HEAD).transpose(2, 4, 3, 0, 1)
    return s.reshape(n_ih, RWKV_HEAD, i_lo * B * RWKV_HEADS)


def _state_from_tiles(s, B, i_lo):
    n_ih = RWKV_HEAD // i_lo
    s = s.reshape(n_ih, RWKV_HEAD, i_lo, B, RWKV_HEADS).transpose(3, 4, 0, 2, 1)
    return s.reshape(B, RWKV_HEADS, RWKV_HEAD, RWKV_HEAD)


def _outproj_kernel(x_ref, at_ref, cv_ref, y_ref, bonus_ref, g_ref, lg_ref, lb_ref, ones_ref,
                    wa_ref, wc_ref, wr_ref, n1g_ref, n1b_ref, out_ref, *, alpha):
    ones_bd = ones_ref[...]
    y = y_ref[...]
    inv = 1.0 / RWKV_HEAD
    mu = _segsum(y, ones_bd) * inv
    d = y - mu
    var = _segsum(d * d, ones_bd) * inv
    yn = d * lax.rsqrt(var + LNX_EPS) * lg_ref[...] + lb_ref[...]
    rw = ((yn + bonus_ref[...]) * g_ref[...]).astype(BF16)
    h = _dot(at_ref[...], wa_ref[...]) + _dot(cv_ref[...].astype(BF16), wc_ref[...]) + _dot(rw, wr_ref[...])
    out_ref[...] = _layer_norm_rows(alpha * x_ref[...] + h, n1g_ref[...], n1b_ref[...])


def _outproj(x, attn, conv, y, bonus, g, w, alpha):
    n = x.shape[0]
    tm = _tile(n, ROW_TILE)
    row = lambda c: pl.BlockSpec((tm, c), lambda i: (i, 0))
    wts = [w['rwkv_lnx_g'], w['rwkv_lnx_b'], w['ones_bd'], w['wo_a'], w['wo_c'], w['wo_r'], w['ln1_g'], w['ln1_b']]
    return pl.pallas_call(
        functools.partial(_outproj_kernel, alpha=alpha),
        grid=(n // tm,),
        in_specs=[row(D_MODEL), row(MLA_WIDTH), row(CONV_CH), row(RWKV_WIDTH), row(RWKV_WIDTH), row(RWKV_WIDTH)]
                 + [_full(a.shape) for a in wts],
        out_specs=row(D_MODEL),
        out_shape=jax.ShapeDtypeStruct((n, D_MODEL), F32),
        compiler_params=_params("parallel"),
        name="out_proj",
    )(x, attn, conv, y, bonus, g, *wts)


def _norm_ple(x1, f, p, n2g, n2b, wple, wgate, alpha):
    x2 = _layer_norm_rows(alpha * x1 + f, n2g, n2b)
    gate = _sigmoid(_dot(x2.astype(BF16), wgate))
    return x2 + gate * _dot(p.astype(BF16), wple)


def _ffn_kernel(x_ref, p_ref, wg_ref, wu_ref, wd_ref, n2g_ref, n2b_ref, wple_ref, wgate_ref,
                out_ref, xb_ref, acc_ref, *, alpha):
    f = pl.program_id(1)

    @pl.when(f == 0)
    def _():
        xb_ref[...] = x_ref[...].astype(BF16)
        acc_ref[...] = jnp.zeros(acc_ref.shape, F32)

    xb = xb_ref[...]
    gate = _dot(xb, wg_ref[...])
    hid = (gate * _sigmoid(gate) * _dot(xb, wu_ref[...])).astype(BF16)
    acc_ref[...] += _dot(hid, wd_ref[...])

    @pl.when(f == pl.num_programs(1) - 1)
    def _():
        out_ref[...] = _norm_ple(x_ref[...], acc_ref[...], p_ref[...], n2g_ref[...], n2b_ref[...],
                                 wple_ref[...], wgate_ref[...], alpha)


def _ffn(x, p, wg, wu, wd, w, alpha):
    n = x.shape[0]
    d_ff = wg.shape[1]
    tm = _tile(n, ROW_TILE)
    tf = _tile(d_ff, 1408, LANES)
    row = lambda c: pl.BlockSpec((tm, c), lambda i, f: (i, 0))
    wts = [w['ln2_g'], w['ln2_b'], w['ple_w'], w['ple_gate_w']]
    return pl.pallas_call(
        functools.partial(_ffn_kernel, alpha=alpha),
        grid=(n // tm, d_ff // tf),
        in_specs=[row(D_MODEL), row(p.shape[1]),
                  pl.BlockSpec((D_MODEL, tf), lambda i, f: (0, f)),
                  pl.BlockSpec((D_MODEL, tf), lambda i, f: (0, f)),
                  pl.BlockSpec((tf, D_MODEL), lambda i, f: (f, 0))] + [_full(a.shape) for a in wts],
        out_specs=row(D_MODEL),
        out_shape=jax.ShapeDtypeStruct((n, D_MODEL), F32),
        scratch_shapes=[pltpu.VMEM((tm, D_MODEL), BF16), pltpu.VMEM((tm, D_MODEL), F32)],
        compiler_params=_params("parallel", "arbitrary"),
        name="ffn_dense",
    )(x, p, wg, wu, wd, *wts)


def _router_kernel(x_ref, wr_ref, idx_ref, gate_ref):
    xh, xm, _ = _split3(x_ref[...])
    wh, wm, _ = _split3(wr_ref[...])
    lt = _dot_nt(wh, xh) + _dot_nt(wh, xm) + _dot_nt(wm, xh)
    e = lax.broadcasted_iota(jnp.int32, lt.shape, 0).astype(F32)
    lt = jnp.where(e < N_EXPERTS, lt, -jnp.inf)
    m1 = jnp.max(lt, axis=0, keepdims=True)
    i1 = jnp.min(jnp.where(lt == m1, e, float(N_EXPERTS)), axis=0, keepdims=True)
    lt2 = jnp.where(e == i1, -jnp.inf, lt)
    m2 = jnp.max(lt2, axis=0, keepdims=True)
    i2 = jnp.min(jnp.where(lt2 == m2, e, float(N_EXPERTS)), axis=0, keepdims=True)
    d = jnp.exp(m2 - m1)
    idx_ref[0:1, :] = i1.astype(jnp.int32)
    idx_ref[1:2, :] = i2.astype(jnp.int32)
    gate_ref[0:1, :] = 1.0 / (1.0 + d)
    gate_ref[1:2, :] = d / (1.0 + d)


def _router(x, wr_t):
    n = x.shape[0]
    tm = _tile(n, ROW_TILE, LANES)
    return pl.pallas_call(
        _router_kernel,
        grid=(n // tm,),
        in_specs=[pl.BlockSpec((tm, D_MODEL), lambda i: (i, 0)), _full(wr_t.shape)],
        out_specs=[pl.BlockSpec((TOP_K, tm), lambda i: (0, i)), pl.BlockSpec((TOP_K, tm), lambda i: (0, i))],
        out_shape=[jax.ShapeDtypeStruct((TOP_K, n), jnp.int32), jax.ShapeDtypeStruct((TOP_K, n), F32)],
        compiler_params=_params("parallel"),
        name="moe_router",
    )(x, wr_t)


def _row_copy(src_hbm, buf_ref, sem, r, src_row):
    return pltpu.make_async_copy(src_hbm.at[pl.ds(src_row, 1)], buf_ref.at[pl.ds(r, 1)], sem)


def _gather_start(idx_ref, base, src_hbm, buf_ref, sem, n_rows):
    def start(r, c):
        _row_copy(src_hbm, buf_ref, sem, r, idx_ref[base + r]).start()
        return c

    lax.fori_loop(0, n_rows, start, 0)


def _gather_wait(src_hbm, buf_ref, sem, n_rows):
    def wait(r, c):
        _row_copy(src_hbm, buf_ref, sem, r, 0).wait()
        return c

    lax.fori_loop(0, n_rows, wait, 0)


def _moe_gather_kernel(tok_ref, x_hbm, out_ref, buf_ref, sem, *, rows):
    _gather_start(tok_ref, pl.program_id(0) * rows, x_hbm, buf_ref, sem, rows)
    _gather_wait(x_hbm, buf_ref, sem, rows)
    out_ref[...] = buf_ref[...].astype(BF16)


def _moe_gather(row_tok, x):
    n_rows = row_tok.shape[0]
    rows = _tile(n_rows, GATHER_ROWS, 16)
    grid_spec = pltpu.PrefetchScalarGridSpec(
        num_scalar_prefetch=1,
        grid=(n_rows // rows,),
        in_specs=[pl.BlockSpec(memory_space=pl.ANY)],
        out_specs=pl.BlockSpec((rows, D_MODEL), lambda i, tok: (i, 0)),
        scratch_shapes=[pltpu.VMEM((rows, D_MODEL), F32), pltpu.SemaphoreType.DMA],
    )
    return pl.pallas_call(
        functools.partial(_moe_gather_kernel, rows=rows),
        grid_spec=grid_spec,
        out_shape=jax.ShapeDtypeStruct((n_rows, D_MODEL), BF16),
        compiler_params=_params("arbitrary"),
        name="moe_gather",
    )(row_tok, x)


def _moe_ffn_kernel(be_ref, x_ref, wg_ref, wu_ref, wd_ref, out_ref, acc_ref):
    del be_ref
    f = pl.program_id(1)

    @pl.when(f == 0)
    def _():
        acc_ref[...] = jnp.zeros(acc_ref.shape, F32)

    xb = x_ref[...]
    gate = _dot(xb, wg_ref[0])
    hid = (gate * _sigmoid(gate) * _dot(xb, wu_ref[0])).astype(BF16)
    acc_ref[...] += _dot(hid, wd_ref[0])

    @pl.when(f == pl.num_programs(1) - 1)
    def _():
        out_ref[...] = acc_ref[...]


def _moe_ffn(block_e, xs, wg, wu, wd, m_rows):
    n_rows = xs.shape[0]
    d_ff = wg.shape[2]
    tf = _tile(d_ff, 896, LANES)
    grid_spec = pltpu.PrefetchScalarGridSpec(
        num_scalar_prefetch=1,
        grid=(n_rows // m_rows, d_ff // tf),
        in_specs=[pl.BlockSpec((m_rows, D_MODEL), lambda i, f, be: (i, 0)),
                  pl.BlockSpec((1, D_MODEL, tf), lambda i, f, be: (be[i], 0, f)),
                  pl.BlockSpec((1, D_MODEL, tf), lambda i, f, be: (be[i], 0, f)),
                  pl.BlockSpec((1, tf, D_MODEL), lambda i, f, be: (be[i], f, 0))],
        out_specs=pl.BlockSpec((m_rows, D_MODEL), lambda i, f, be: (i, 0)),
        scratch_shapes=[pltpu.VMEM((m_rows, D_MODEL), F32)],
    )
    return pl.pallas_call(
        _moe_ffn_kernel,
        grid_spec=grid_spec,
        out_shape=jax.ShapeDtypeStruct((n_rows, D_MODEL), F32),
        compiler_params=_params("parallel", "arbitrary"),
        name="moe_ffn",
    )(block_e, xs, wg, wu, wd)


def _moe_combine_kernel(pos0_ref, pos1_ref, rows_hbm, x_ref, p_ref, gate_ref, n2g_ref, n2b_ref, wple_ref, wgate_ref,
                        out_ref, b0_ref, b1_ref, sem0, sem1, *, tm, alpha):
    base = pl.program_id(0) * tm
    _gather_start(pos0_ref, base, rows_hbm, b0_ref, sem0, tm)
    _gather_start(pos1_ref, base, rows_hbm, b1_ref, sem1, tm)
    _gather_wait(rows_hbm, b0_ref, sem0, tm)
    _gather_wait(rows_hbm, b1_ref, sem1, tm)
    gts = gate_ref[...]
    f = b0_ref[...] * gts[:, 0:1] + b1_ref[...] * gts[:, 1:2]
    out_ref[...] = _norm_ple(x_ref[...], f, p_ref[...], n2g_ref[...], n2b_ref[...],
                             wple_ref[...], wgate_ref[...], alpha)


def _moe_combine(pos0, pos1, out_rows, x, p, gates, w, alpha):
    n = x.shape[0]
    tm = _tile(n, GATHER_ROWS)
    row = lambda c: pl.BlockSpec((tm, c), lambda i, a, b: (i, 0))
    wts = [w['ln2_g'], w['ln2_b'], w['ple_w'], w['ple_gate_w']]
    grid_spec = pltpu.PrefetchScalarGridSpec(
        num_scalar_prefetch=2,
        grid=(n // tm,),
        in_specs=[pl.BlockSpec(memory_space=pl.ANY), row(D_MODEL), row(p.shape[1]), row(TOP_K)]
                 + [_full(a.shape) for a in wts],
        out_specs=row(D_MODEL),
        scratch_shapes=[pltpu.VMEM((tm, D_MODEL), F32), pltpu.VMEM((tm, D_MODEL), F32),
                        pltpu.SemaphoreType.DMA, pltpu.SemaphoreType.DMA],
    )
    return pl.pallas_call(
        functools.partial(_moe_combine_kernel, tm=tm, alpha=alpha),
        grid_spec=grid_spec,
        out_shape=jax.ShapeDtypeStruct((n, D_MODEL), F32),
        compiler_params=_params("arbitrary"),
        name="moe_combine",
    )(pos0, pos1, out_rows, x, p, gates, *wts)


def _moe(x, p, wr_t, wg, wu, wd, w, alpha):
    n = x.shape[0]
    idx, gates = _router(x, wr_t)
    flat_e = idx.T.reshape(-1)
    onehot = (flat_e[:, None] == jnp.arange(N_EXPERTS, dtype=jnp.int32)[None, :]).astype(jnp.int32)
    csum = jnp.cumsum(onehot, axis=0)
    rank = jnp.take_along_axis(csum, flat_e[:, None], axis=1)[:, 0] - 1
    counts = csum[-1]
    m_rows = min(MOE_ROWS, max(SUBLANES * 2, n))
    padded = (counts + m_rows - 1) // m_rows * m_rows
    pad_end = jnp.cumsum(padded)
    dest = (pad_end - padded)[flat_e] + rank
    n_assign = n * TOP_K
    n_blocks = -(-n_assign // m_rows) + N_EXPERTS
    flat_tok = jnp.repeat(jnp.arange(n, dtype=jnp.int32), TOP_K)
    row_tok = jnp.zeros((n_blocks * m_rows,), jnp.int32).at[dest].set(flat_tok)
    block_e = jnp.minimum(jnp.searchsorted(pad_end, jnp.arange(n_blocks, dtype=jnp.int32) * m_rows, side='right'),
                          N_EXPERTS - 1).astype(jnp.int32)
    xs = _moe_gather(row_tok, x)
    out_rows = _moe_ffn(block_e, xs, wg, wu, wd, m_rows)
    dest2 = dest.reshape(n, TOP_K)
    return _moe_combine(dest2[:, 0], dest2[:, 1], out_rows, x, p, gates.T, w, alpha)


def _rot_cols(wmat):
    half = MLA_ROPE // 2
    return jnp.concatenate([-wmat[..., half:], wmat[..., :half]], axis=-1)


def _layer_weights(l, a):
    o1 = Q_LORA
    o2 = o1 + KV_LORA
    o3 = o2 + MLA_ROPE
    o4 = o3 + 2 * CONV_CH
    w_in = a['w_in'][l]
    w = {}
    w['wq'] = w_in[:, :o1].astype(BF16)
    w['wkv'] = w_in[:, o1:o2].astype(BF16)
    wkr = w_in[:, o2:o3]
    w['wkr8'] = jnp.tile(wkr, (1, MLA_HEADS)).astype(BF16)
    w['wkrr8'] = jnp.tile(_rot_cols(wkr), (1, MLA_HEADS)).astype(BF16)
    w['wcv'] = w_in[:, o3:o4].astype(BF16)
    w['wrw'] = w_in[:, o4:].astype(BF16)
    w['gq'] = a['q_norm_g'][l][None, :]
    w['gkv'] = a['kv_norm_g'][l][None, :]
    wuq = a['w_uq'][l].reshape(Q_LORA, MLA_HEADS, MLA_NOPE + MLA_ROPE)
    w['wuqn'] = wuq[:, :, :MLA_NOPE].reshape(Q_LORA, MLA_HEADS * MLA_NOPE).astype(BF16)
    wuqr = wuq[:, :, MLA_NOPE:]
    w['wuqr'] = wuqr.reshape(Q_LORA, MLA_HEADS * MLA_ROPE).astype(BF16)
    w['wuqrr'] = _rot_cols(wuqr).reshape(Q_LORA, MLA_HEADS * MLA_ROPE).astype(BF16)
    wuk_t = a['w_uk'][l].transpose(1, 2, 0)
    zeros = jnp.zeros_like(wuk_t)
    even = jnp.concatenate([wuk_t, zeros], axis=1)
    odd = jnp.concatenate([zeros, wuk_t], axis=1)
    is_odd = (jnp.arange(MLA_HEADS) % 2 == 1)[:, None, None]
    w['wukp'] = jnp.where(is_odd, odd, even).astype(BF16)
    wuv = a['w_uv'][l].transpose(1, 0, 2)
    zv = jnp.zeros((MLA_HEADS // 2, KV_LORA, MLA_V), F32)
    top = jnp.concatenate([wuv[0::2], zv], axis=2)
    bot = jnp.concatenate([zv, wuv[1::2]], axis=2)
    w['wuvp'] = jnp.concatenate([top, bot], axis=1).astype(BF16)
    w['mla_out_g'] = a['mla_out_g'][l][None, :]
    w['conv_w'] = a['conv_w'][l]
    for name in ('conv_b', 'conv_ln_g', 'conv_ln_b', 'rwkv_mu', 'rwkv_w0', 'rwkv_a0', 'rwkv_k_k', 'rwkv_k_a',
                 'rwkv_lnx_g', 'rwkv_lnx_b', 'ln1_g', 'ln1_b', 'ln2_g', 'ln2_b'):
        w[name] = a[name][l][None, :]
    w['rwkv_r_k'] = a['rwkv_r_k'][l].reshape(1, RWKV_WIDTH)
    pad = lambda m, lo: jnp.zeros((LANES, RWKV_WIDTH), F32).at[lo:lo + m.shape[0]].set(m).astype(BF16)
    w['w2p'] = pad(a['rwkv_w2'][l], 0)
    w['a2p'] = pad(a['rwkv_a2'][l], W_LORA)
    w['g2p'] = pad(a['rwkv_g2'][l], W_LORA + A_LORA)
    head_of = jnp.arange(RWKV_WIDTH) // RWKV_HEAD
    w['ones_bd'] = (head_of[:, None] == head_of[None, :]).astype(BF16)
    w_out = a['w_out'][l]
    w['wo_a'] = w_out[:MLA_WIDTH].astype(BF16)
    w['wo_c'] = w_out[MLA_WIDTH:MLA_WIDTH + CONV_CH].astype(BF16)
    w['wo_r'] = w_out[MLA_WIDTH + CONV_CH:].astype(BF16)
    w['ple_w'] = a['ple_w'][l].astype(BF16)
    w['ple_gate_w'] = a['ple_gate_w'][l].astype(BF16)
    return w


def _rope_tables(pos):
    half = MLA_ROPE // 2
    inv = ROPE_BASE ** (-jnp.arange(half, dtype=F32) / half)
    ang = pos.astype(F32)[:, None] * inv[None, :]
    cos = jnp.cos(ang)
    sin = jnp.sin(ang)
    rep = lambda t: jnp.tile(jnp.concatenate([t, t], axis=-1), (1, MLA_HEADS))
    return rep(cos), rep(sin)


def _run_group(x, p_emb, pos, a, layer_w, mixers_w, depth, alpha, past):
    B, T, _ = x.shape
    n = B * T
    x = x.reshape(n, D_MODEL)
    cos_t, sin_t = _rope_tables(pos)
    chains = B * RWKV_HEADS
    assert LANES % chains == 0 and RWKV_HEAD % (LANES // chains) == 0
    i_lo = LANES // chains
    news = []
    for l in range(depth):
        w = layer_w[l]
        q, kvb, ckv, kr, u, rcols = _proj(x, cos_t, sin_t, w, T)
        if past is None:
            o = _attn_prompt(q, kvb, B, T)
            conv_buf = jnp.zeros((B, CONV_W - 1, CONV_CH), F32)
            shift_buf = jnp.zeros((B, RWKV_COLS), F32)
            wkv0 = jnp.zeros((B, RWKV_HEADS, RWKV_HEAD, RWKV_HEAD), F32)
        else:
            rows = MLA_HEADS * T
            q4 = q.reshape(MLA_HEADS, B, T, QK_WIDTH).transpose(1, 0, 2, 3)
            ql = q4[..., :KV_LORA].reshape(B, rows, KV_LORA)
            qr = jnp.stack([q4[:, h, :, KV_LORA + h * MLA_ROPE:KV_LORA + (h + 1) * MLA_ROPE]
                            for h in range(MLA_HEADS)], axis=1).reshape(B, rows, MLA_ROPE)
            o = _attn_sample(ql, qr, ckv.reshape(B, T, KV_LORA), kr.reshape(B, T, MLA_ROPE),
                             past['cache_ckv'], past['cache_kr'], past['page_table'], l)
            o = o.reshape(B, MLA_HEADS, T, KV_LORA).transpose(1, 0, 2, 3).reshape(MLA_HEADS, n, KV_LORA).astype(BF16)
            conv_buf, shift_buf, wkv0 = past['state_conv'][l], past['state_shift'][l], past['state_wkv'][l]
        attn = _uvnorm(o, w['wuvp'], w['mla_out_g'])
        conv, conv_new = _conv(u.reshape(B, T, CONV_CH), conv_buf, w['conv_w'], w['conv_b'],
                               w['conv_ln_g'], w['conv_ln_b'])
        rc3 = rcols.reshape(B, T, RWKV_COLS)
        shifted = jnp.concatenate([shift_buf[:, None, :], rc3[:, :-1]], axis=1).reshape(n, RWKV_COLS)
        r, dec, nkk, kka, kp, v, g, bonus = _rwkv_prep(rcols, shifted, w)
        ops = [_to_key_tiles(t, B, T, i_lo) for t in (dec, nkk, kka, kp, r)]
        y_rows, s_fin = _scan(ops, _to_val_rows(v, B, T, i_lo), _state_to_tiles(wkv0, i_lo))
        y = _from_val_rows(y_rows, B, T, i_lo)
        x1 = _outproj(x, attn, conv.reshape(n, CONV_CH), y, bonus, g, w, alpha)
        j = l // 2
        p_l = p_emb[l].reshape(n, -1)
        if l % 2 == 0:
            x = _ffn(x1, p_l, mixers_w['ffn_g'][j], mixers_w['ffn_u'][j], mixers_w['ffn_d'][j], w, alpha)
        else:
            x = _moe(x1, p_l, mixers_w['router_t'][j], mixers_w['moe_g'][j], mixers_w['moe_u'][j],
                     mixers_w['moe_d'][j], w, alpha)
        news.append((ckv.reshape(B, T, KV_LORA), kr.reshape(B, T, MLA_ROPE), conv_new, rc3[:, -1],
                     _state_from_tiles(s_fin, B, i_lo)))
    outs = tuple(jnp.stack([st[i] for st in news]) for i in range(5))
    return (x.reshape(B, T, D_MODEL),) + outs


def kernel(x_prompt, x_sample, cache_ckv, cache_kr, state_conv, state_shift, state_wkv, page_table,
           p_prompt, p_sample, w_in, q_norm_g, w_uq, kv_norm_g, w_uk, w_uv, mla_out_g,
           conv_w, conv_b, conv_ln_g, conv_ln_b, rwkv_mu, rwkv_w0, rwkv_w2, rwkv_a0, rwkv_a2,
           rwkv_g2, rwkv_k_k, rwkv_k_a, rwkv_r_k, rwkv_lnx_g, rwkv_lnx_b, w_out, ln1_g, ln1_b,
           ffn_w_gate, ffn_w_up, ffn_w_down, moe_router, moe_w_gate, moe_w_up, moe_w_down,
           ln2_g, ln2_b, ple_w, ple_gate_w):
    a = dict(w_in=w_in, q_norm_g=q_norm_g, w_uq=w_uq, kv_norm_g=kv_norm_g, w_uk=w_uk, w_uv=w_uv,
             mla_out_g=mla_out_g, conv_w=conv_w, conv_b=conv_b, conv_ln_g=conv_ln_g, conv_ln_b=conv_ln_b,
             rwkv_mu=rwkv_mu, rwkv_w0=rwkv_w0, rwkv_w2=rwkv_w2, rwkv_a0=rwkv_a0, rwkv_a2=rwkv_a2,
             rwkv_g2=rwkv_g2, rwkv_k_k=rwkv_k_k, rwkv_k_a=rwkv_k_a, rwkv_r_k=rwkv_r_k,
             rwkv_lnx_g=rwkv_lnx_g, rwkv_lnx_b=rwkv_lnx_b, w_out=w_out, ln1_g=ln1_g, ln1_b=ln1_b,
             ln2_g=ln2_g, ln2_b=ln2_b, ple_w=ple_w, ple_gate_w=ple_gate_w)
    depth = w_in.shape[0]
    alpha = (2 * depth) ** 0.25
    layer_w = [_layer_weights(l, a) for l in range(depth)]
    mixers_w = dict(ffn_g=ffn_w_gate.astype(BF16), ffn_u=ffn_w_up.astype(BF16), ffn_d=ffn_w_down.astype(BF16),
                    router_t=jnp.pad(moe_router.transpose(0, 2, 1), ((0, 0), (0, ROUTER_ROWS - N_EXPERTS), (0, 0))),
                    moe_g=moe_w_gate.astype(BF16), moe_u=moe_w_up.astype(BF16), moe_d=moe_w_down.astype(BF16))
    n_pages = page_table.shape[1]
    past_len = n_pages * PAGE_SIZE
    pos_prompt = jnp.arange(x_prompt.shape[1], dtype=jnp.int32)
    pos_sample = past_len + jnp.arange(x_sample.shape[1], dtype=jnp.int32)
    past = dict(cache_ckv=cache_ckv, cache_kr=cache_kr, page_table=page_table,
                state_conv=state_conv, state_shift=state_shift, state_wkv=state_wkv)
    yp = _run_group(x_prompt, p_prompt, pos_prompt, a, layer_w, mixers_w, depth, alpha, None)
    ys = _run_group(x_sample, p_sample, pos_sample, a, layer_w, mixers_w, depth, alpha, past)
    return (yp[0], ys[0]) + yp[1:] + ys[1:]
```

```python
import functools

import jax
import jax.numpy as jnp
from jax import lax
from jax.experimental import pallas as pl
from jax.experimental.pallas import tpu as pltpu

F32 = jnp.float32
BF16 = jnp.bfloat16

D_MODEL = 1024
MLA_HEADS = 8
MLA_NOPE = 64
MLA_ROPE = 32
MLA_V = 64
MLA_WIDTH = MLA_HEADS * MLA_V
Q_LORA = 384
KV_LORA = 256
ROPE_BASE = 10000.0
MLA_SCALE = (MLA_NOPE + MLA_ROPE) ** -0.5
CONV_CH = 256
CONV_W = 31
RWKV_WIDTH = 256
RWKV_HEAD = 64
RWKV_HEADS = 4
W_LORA = 32
A_LORA = 32
G_LORA = 64
RWKV_COLS = 3 * RWKV_WIDTH + W_LORA + A_LORA + G_LORA
LNX_EPS = 64e-5
N_EXPERTS = 8
TOP_K = 2
PAGE_SIZE = 128
LN_EPS = 1e-5
RMS_EPS = 1e-6

LANES = 128
SUBLANES = 8
VMEM_LIMIT_BYTES = 56 * 1024 * 1024
QK_WIDTH = 2 * KV_LORA

ROW_TILE = 512
ATTN_TQ = 128
ATTN_TK = 512
ATTN_ROW_CHUNK = 64
LOG2E = 1.4426950408889634
Q_SCALE = MLA_SCALE * LOG2E
PAGES_PER_STEP = 16
NEW_ROWS_PAD = 16
SCAN_STEPS = 64
MOE_ROWS = 512
GATHER_ROWS = 256
GATHER_UNROLL = 8
ROUTER_ROWS = 16


def _tile(n, pref, mult=SUBLANES):
    if n <= pref:
        return n
    for t in range(pref, 0, -1):
        if n % t == 0 and t % mult == 0:
            return t
    return n


def _params(*sem):
    return pltpu.CompilerParams(dimension_semantics=sem, vmem_limit_bytes=VMEM_LIMIT_BYTES)


def _dot(a, b):
    return jnp.dot(a, b, preferred_element_type=F32)


def _dot_nt(a, b):
    return lax.dot_general(a, b, (((1,), (1,)), ((), ())), preferred_element_type=F32)


def _split3(x):
    hi = x.astype(BF16)
    r1 = x - hi.astype(F32)
    mid = r1.astype(BF16)
    lo = (r1 - mid.astype(F32)).astype(BF16)
    return hi, mid, lo


def _segsum(x, ones_bd):
    hi, mid, lo = _split3(x)
    return _dot(hi, ones_bd) + _dot(mid, ones_bd) + _dot(lo, ones_bd)


def _layer_norm_rows(z, g, b):
    mu = jnp.mean(z, axis=-1, keepdims=True)
    d = z - mu
    var = jnp.mean(d * d, axis=-1, keepdims=True)
    return d * lax.rsqrt(var + LN_EPS) * g + b


def _rms_norm_rows(z, g):
    return z * lax.rsqrt(jnp.mean(z * z, axis=-1, keepdims=True) + RMS_EPS) * g


def _sigmoid(z):
    return 1.0 / (1.0 + jnp.exp(-z))


def _full(shape):
    nd = len(shape)
    return pl.BlockSpec(shape, lambda *_: (0,) * nd)


def _proj_kernel(x_ref, cos_ref, sin_ref, wq_ref, wkv_ref, wkr_ref, wkrr_ref, wcv_ref, wrw_ref,
                 gq_ref, gkv_ref, wuqn_ref, wuqr_ref, wuqrr_ref, wukp_ref,
                 q_ref, kvb_ref, ckv_ref, kr_ref, u_ref, rc_ref):
    xb = x_ref[...].astype(BF16)
    cos = cos_ref[...]
    sin = sin_ref[...]
    cq = _rms_norm_rows(_dot(xb, wq_ref[...]), gq_ref[...]).astype(BF16)
    qn = _dot(cq, wuqn_ref[...]).astype(BF16)
    qr = (_dot(cq, wuqr_ref[...]) * cos + _dot(cq, wuqrr_ref[...]) * sin) * Q_SCALE
    lane = lax.broadcasted_iota(jnp.int32, qr.shape, 1)
    for h in range(MLA_HEADS):
        slab = qn[:, (h // 2) * LANES:(h // 2 + 1) * LANES]
        q_ref[h, :, 0:KV_LORA] = (_dot(slab, wukp_ref[h]) * Q_SCALE).astype(BF16)
        own = (lane >= h * MLA_ROPE) & (lane < (h + 1) * MLA_ROPE)
        q_ref[h, :, KV_LORA:QK_WIDTH] = jnp.where(own, qr, 0.0).astype(BF16)
    ckv = _rms_norm_rows(_dot(xb, wkv_ref[...]), gkv_ref[...])
    ckv_ref[...] = ckv
    kvb_ref[:, 0:KV_LORA] = ckv.astype(BF16)
    kr8 = _dot(xb, wkr_ref[...]) * cos + _dot(xb, wkrr_ref[...]) * sin
    kvb_ref[:, KV_LORA:QK_WIDTH] = kr8.astype(BF16)
    kr_ref[...] = kr8[:, 0:MLA_ROPE]
    cv = _dot(xb, wcv_ref[...])
    u_ref[...] = cv[:, 0:CONV_CH] * _sigmoid(cv[:, CONV_CH:2 * CONV_CH])
    rc_ref[...] = _dot(xb, wrw_ref[...])


def _proj(x, cos_t, sin_t, w, T):
    n = x.shape[0]
    tm = _tile(n, ROW_TILE)
    if T % tm == 0:
        nt = T // tm
        tab_map = lambda i: (i % nt, 0)
    else:
        assert tm % T == 0
        cos_t = jnp.tile(cos_t, (tm // T, 1))
        sin_t = jnp.tile(sin_t, (tm // T, 1))
        tab_map = lambda i: (0, 0)
    row = lambda c: pl.BlockSpec((tm, c), lambda i: (i, 0))
    wts = [w['wq'], w['wkv'], w['wkr8'], w['wkrr8'], w['wcv'], w['wrw'], w['gq'], w['gkv'],
           w['wuqn'], w['wuqr'], w['wuqrr'], w['wukp']]
    return pl.pallas_call(
        _proj_kernel,
        grid=(n // tm,),
        in_specs=[row(D_MODEL), pl.BlockSpec((tm, KV_LORA), tab_map), pl.BlockSpec((tm, KV_LORA), tab_map)]
                 + [_full(a.shape) for a in wts],
        out_specs=[pl.BlockSpec((MLA_HEADS, tm, QK_WIDTH), lambda i: (0, i, 0)),
                   row(QK_WIDTH), row(KV_LORA), row(MLA_ROPE), row(CONV_CH), row(RWKV_COLS)],
        out_shape=[jax.ShapeDtypeStruct((MLA_HEADS, n, QK_WIDTH), BF16),
                   jax.ShapeDtypeStruct((n, QK_WIDTH), BF16),
                   jax.ShapeDtypeStruct((n, KV_LORA), F32),
                   jax.ShapeDtypeStruct((n, MLA_ROPE), F32),
                   jax.ShapeDtypeStruct((n, CONV_CH), F32),
                   jax.ShapeDtypeStruct((n, RWKV_COLS), F32)],
        compiler_params=_params("parallel"),
        name="proj",
    )(x, cos_t, sin_t, *wts)


def _attn_prompt_kernel(q_ref, kv_ref, o_ref, s_ref, p_ref, m_ref, l_ref, a_ref, acc_ref, *, tq, tk, rc):
    qi = pl.program_id(1)
    rows = MLA_HEADS * tq
    q = q_ref[...].reshape(rows, QK_WIDTH)
    m_ref[...] = jnp.full(m_ref.shape, -jnp.inf, F32)
    l_ref[...] = jnp.zeros(l_ref.shape, F32)
    acc_ref[...] = jnp.zeros(acc_ref.shape, F32)

    half = rows // 2

    def block(k0, width, masked):
        kv = kv_ref[pl.ds(k0, width), :]
        s_ref[0:half, 0:width] = _dot_nt(q[0:half], kv)
        s_ref[half:rows, 0:width] = _dot_nt(q[half:rows], kv)

        def scores(c):
            s = s_ref[pl.ds(c * rc, rc), 0:width]
            if masked:
                q_pos = qi * tq + ((c * rc + lax.broadcasted_iota(jnp.int32, (rc, 1), 0)) & (tq - 1))
                k_pos = k0 + lax.broadcasted_iota(jnp.int32, (1, width), 1)
                s = jnp.where(k_pos <= q_pos, s, -jnp.inf)
            return [s[:, g * LANES:(g + 1) * LANES] for g in range(width // LANES)]

        for c in range(rows // rc):
            sl = pl.ds(c * rc, rc)
            mx = functools.reduce(jnp.maximum, scores(c))
            m_prev = m_ref[sl, :]
            m_new = jnp.maximum(m_prev, jnp.max(mx, axis=-1, keepdims=True))
            a_ref[sl, :] = jnp.exp2(m_prev - m_new)
            m_ref[sl, :] = m_new
        for c in range(rows // rc):
            sl = pl.ds(c * rc, rc)
            m_cur = m_ref[sl, :]
            ps = [jnp.exp2(g - m_cur) for g in scores(c)]
            alpha = a_ref[sl, :]
            l_ref[sl, :] = alpha * l_ref[sl, :] + functools.reduce(jnp.add, ps)
            p_ref[sl, 0:width] = jnp.concatenate(ps, axis=-1).astype(BF16)
            acc_ref[sl, :] = acc_ref[sl, :] * jnp.concatenate([alpha] * (KV_LORA // LANES), axis=-1)
        vals = kv[:, 0:KV_LORA]
        acc_ref[0:half, :] += _dot(p_ref[0:half, 0:width], vals)
        acc_ref[half:rows, :] += _dot(p_ref[half:rows, 0:width], vals)

    q0 = qi * tq
    n_full = q0 // tk

    def full_block(kb, carry):
        block(pl.multiple_of(kb * tk, tk), tk, False)
        return carry

    lax.fori_loop(0, n_full, full_block, 0)
    lead = (q0 - n_full * tk) // tq
    for r in range(tk // tq):
        @pl.when(lead == r)
        def _(r=r):
            block(pl.multiple_of(n_full * tk, tk), (r + 1) * tq, True)

    o = acc_ref[...] / jnp.sum(l_ref[...], axis=-1, keepdims=True)
    o_ref[...] = o.reshape(MLA_HEADS, tq, KV_LORA).astype(BF16)


def _attn_prompt(q, kvb, B, T):
    n = B * T
    tq = _tile(T, ATTN_TQ, 16)
    tk = _tile(T, ATTN_TK, 16)
    assert tq & (tq - 1) == 0 and tk % tq == 0
    nq = T // tq
    rows = MLA_HEADS * tq
    rc = _tile(rows, ATTN_ROW_CHUNK, 16)
    return pl.pallas_call(
        functools.partial(_attn_prompt_kernel, tq=tq, tk=tk, rc=rc),
        grid=(B, nq),
        in_specs=[pl.BlockSpec((MLA_HEADS, tq, QK_WIDTH), lambda b, i: (0, b * nq + i, 0)),
                  pl.BlockSpec((T, QK_WIDTH), lambda b, i: (b, 0))],
        out_specs=pl.BlockSpec((MLA_HEADS, tq, KV_LORA), lambda b, i: (0, b * nq + i, 0)),
        out_shape=jax.ShapeDtypeStruct((MLA_HEADS, n, KV_LORA), BF16),
        scratch_shapes=[pltpu.VMEM((rows, tk), F32), pltpu.VMEM((rows, tk), BF16),
                        pltpu.VMEM((rows, LANES), F32), pltpu.VMEM((rows, LANES), F32),
                        pltpu.VMEM((rows, LANES), F32), pltpu.VMEM((rows, KV_LORA), F32)],
        compiler_params=_params("parallel", "parallel"),
        name="attn_prompt",
    )(q, kvb)


def _attn_sample_kernel(pt_ref, ql_ref, qr_ref, *refs, n_pg, n_new):
    del pt_ref
    c_pages = refs[:n_pg]
    r_pages = refs[n_pg:2 * n_pg]
    cn_ref, rn_ref, o_ref, m_ref, l_ref, acc_ref = refs[2 * n_pg:]
    g = pl.program_id(1)

    @pl.when(g == 0)
    def _():
        m_ref[...] = jnp.full(m_ref.shape, -jnp.inf, F32)
        l_ref[...] = jnp.zeros(l_ref.shape, F32)
        acc_ref[...] = jnp.zeros(acc_ref.shape, F32)

    ql = ql_ref[0]
    qr = qr_ref[0]

    def update(s, vals):
        m_prev = m_ref[...]
        m_new = jnp.maximum(m_prev, jnp.max(s, axis=-1, keepdims=True))
        p = jnp.exp2(s - m_new)
        alpha = jnp.exp2(m_prev - m_new)
        l_ref[...] = alpha * l_ref[...] + jnp.sum(p, axis=-1, keepdims=True)
        acc_ref[...] = alpha * acc_ref[...] + _dot(p.astype(BF16), vals)
        m_ref[...] = m_new

    cs = [c_pages[k][0, 0].astype(BF16) for k in range(n_pg)]
    s = jnp.concatenate([_dot_nt(ql, cs[k]) + _dot(qr, r_pages[k][0, 0].astype(BF16))
                         for k in range(n_pg)], axis=-1)
    update(s, jnp.concatenate(cs, axis=0))

    @pl.when(g == pl.num_programs(1) - 1)
    def _():
        cn = cn_ref[0].astype(BF16)
        s_new = _dot_nt(ql, cn) + _dot_nt(qr, rn_ref[0].astype(BF16))
        t_q = lax.broadcasted_iota(jnp.int32, s_new.shape, 0) & (n_new - 1)
        t_k = lax.broadcasted_iota(jnp.int32, s_new.shape, 1)
        update(jnp.where(t_k <= t_q, s_new, -jnp.inf), cn)
        o_ref[0] = acc_ref[...] / l_ref[...]


def _attn_sample(ql, qr, ckv_new, kr_new, cache_ckv, cache_kr_t, page_table, layer):
    B, rows, _ = ql.shape
    n_new = ckv_new.shape[1]
    assert n_new & (n_new - 1) == 0
    n_pad = max(NEW_ROWS_PAD, n_new)
    ckv_new = jnp.pad(ckv_new, ((0, 0), (0, n_pad - n_new), (0, 0)))
    kr_new = jnp.pad(kr_new, ((0, 0), (0, n_pad - n_new), (0, 0)))
    n_pages = page_table.shape[1]
    n_pg = _tile(n_pages, PAGES_PER_STEP, 1)
    n_g = n_pages // n_pg

    def page_spec(shape, k):
        return pl.BlockSpec((1, 1) + shape, lambda b, g, pt: (layer, pt[b * n_pages + g * n_pg + k], 0, 0))

    grid_spec = pltpu.PrefetchScalarGridSpec(
        num_scalar_prefetch=1,
        grid=(B, n_g),
        in_specs=[pl.BlockSpec((1, rows, KV_LORA), lambda b, g, pt: (b, 0, 0)),
                  pl.BlockSpec((1, rows, MLA_ROPE), lambda b, g, pt: (b, 0, 0))]
                 + [page_spec((PAGE_SIZE, KV_LORA), k) for k in range(n_pg)]
                 + [page_spec((MLA_ROPE, PAGE_SIZE), k) for k in range(n_pg)]
                 + [pl.BlockSpec((1, n_pad, KV_LORA), lambda b, g, pt: (b, 0, 0)),
                    pl.BlockSpec((1, n_pad, MLA_ROPE), lambda b, g, pt: (b, 0, 0))],
        out_specs=pl.BlockSpec((1, rows, KV_LORA), lambda b, g, pt: (b, 0, 0)),
        scratch_shapes=[pltpu.VMEM((rows, 1), F32), pltpu.VMEM((rows, 1), F32),
                        pltpu.VMEM((rows, KV_LORA), F32)],
    )
    return pl.pallas_call(
        functools.partial(_attn_sample_kernel, n_pg=n_pg, n_new=n_new),
        grid_spec=grid_spec,
        out_shape=jax.ShapeDtypeStruct((B, rows, KV_LORA), F32),
        compiler_params=_params("parallel", "arbitrary"),
        name="attn_sample",
    )(page_table.reshape(-1), ql, qr, *([cache_ckv] * n_pg), *([cache_kr_t] * n_pg), ckv_new, kr_new)


def _uvnorm_kernel(o_ref, wuvp_ref, g_ref, out_ref):
    parts = []
    for p in range(MLA_HEADS // 2):
        pair = jnp.concatenate([o_ref[2 * p], o_ref[2 * p + 1]], axis=-1)
        parts.append(_dot(pair, wuvp_ref[p]))
    a = jnp.concatenate(parts, axis=-1)
    out_ref[...] = _rms_norm_rows(a, g_ref[...]).astype(BF16)


def _uvnorm(o, wuvp, g):
    n = o.shape[1]
    tm = _tile(n, ROW_TILE)
    return pl.pallas_call(
        _uvnorm_kernel,
        grid=(n // tm,),
        in_specs=[pl.BlockSpec((MLA_HEADS, tm, KV_LORA), lambda i: (0, i, 0)), _full(wuvp.shape), _full(g.shape)],
        out_specs=pl.BlockSpec((tm, MLA_WIDTH), lambda i: (i, 0)),
        out_shape=jax.ShapeDtypeStruct((n, MLA_WIDTH), BF16),
        compiler_params=_params("parallel"),
        name="uv_norm",
    )(o, wuvp, g)


CONV_PAD = 32


def _conv_kernel(u_ref, buf_ref, w_ref, b_ref, g_ref, bb_ref, y_ref, new_ref, up_ref, *, T, tt):
    lead = CONV_PAD - (CONV_W - 1)
    up_ref[lead:CONV_PAD, :] = buf_ref[0]
    up_ref[CONV_PAD:CONV_PAD + T, :] = u_ref[0]
    for t0 in range(0, T, tt):
        acc = jnp.zeros((tt, CONV_CH), F32) + b_ref[...]
        for k in range(CONV_W):
            acc = acc + up_ref[lead + t0 + k:lead + t0 + k + tt, :] * w_ref[k:k + 1, :]
        z = _layer_norm_rows(acc, g_ref[...], bb_ref[...])
        y_ref[0, t0:t0 + tt, :] = z * _sigmoid(z)
    new_ref[0] = up_ref[lead + T:CONV_PAD + T, :]


def _conv(u, buf, w, b, g, bb):
    B, T, _ = u.shape
    tt = _tile(T, 256)
    return pl.pallas_call(
        functools.partial(_conv_kernel, T=T, tt=tt),
        grid=(B,),
        in_specs=[pl.BlockSpec((1, T, CONV_CH), lambda i: (i, 0, 0)),
                  pl.BlockSpec((1, CONV_W - 1, CONV_CH), lambda i: (i, 0, 0)),
                  _full(w.shape), _full(b.shape), _full(g.shape), _full(bb.shape)],
        out_specs=[pl.BlockSpec((1, T, CONV_CH), lambda i: (i, 0, 0)),
                   pl.BlockSpec((1, CONV_W - 1, CONV_CH), lambda i: (i, 0, 0))],
        out_shape=[jax.ShapeDtypeStruct((B, T, CONV_CH), F32),
                   jax.ShapeDtypeStruct((B, CONV_W - 1, CONV_CH), F32)],
        scratch_shapes=[pltpu.VMEM((CONV_PAD + T, CONV_CH), F32)],
        compiler_params=_params("parallel"),
        name="conv_module",
    )(u, buf, w, b, g, bb)


def _rwkv_prep_kernel(c_ref, s_ref, mu_ref, w0_ref, a0_ref, kk_ref, ka_ref, rk_ref,
                      w2_ref, a2_ref, g2_ref, ones_ref,
                      r_out, w_out, nkk_out, kka_out, k_out, v_out, g_out, bonus_out):
    cols = c_ref[...]
    xm = cols + (s_ref[...] - cols) * mu_ref[...]
    W = RWKV_WIDTH
    r = xm[:, 0:W]
    k = xm[:, W:2 * W]
    v = xm[:, 2 * W:3 * W]
    tail = xm[:, 3 * W:3 * W + LANES]
    ones_bd = ones_ref[...]
    z = w0_ref[...] + _dot(jnp.tanh(tail).astype(BF16), w2_ref[...])
    softplus = jnp.maximum(-z, 0.0) + jnp.log(1.0 + jnp.exp(-jnp.abs(z)))
    decay = jnp.exp(-jnp.exp(-softplus - 0.5))
    a = _sigmoid(a0_ref[...] + _dot(tail.astype(BF16), a2_ref[...]))
    g = _dot(_sigmoid(tail).astype(BF16), g2_ref[...])
    kk = k * kk_ref[...]
    norm = jnp.sqrt(_segsum(kk * kk, ones_bd))
    kk = kk / jnp.maximum(norm, 1e-12)
    kp = k * (1.0 + (a - 1.0) * ka_ref[...])
    r_out[...] = r
    w_out[...] = decay
    nkk_out[...] = -kk
    kka_out[...] = kk * a
    k_out[...] = kp
    v_out[...] = v
    g_out[...] = g
    bonus_out[...] = _segsum(r * kp * rk_ref[...], ones_bd) * v


def _rwkv_prep(cols, shifted, w):
    n = cols.shape[0]
    tm = _tile(n, ROW_TILE)
    row = lambda c: pl.BlockSpec((tm, c), lambda i: (i, 0))
    wts = [w['rwkv_mu'], w['rwkv_w0'], w['rwkv_a0'], w['rwkv_k_k'], w['rwkv_k_a'], w['rwkv_r_k'],
           w['w2p'], w['a2p'], w['g2p'], w['ones_bd']]
    return pl.pallas_call(
        _rwkv_prep_kernel,
        grid=(n // tm,),
        in_specs=[row(RWKV_COLS), row(RWKV_COLS)] + [_full(a.shape) for a in wts],
        out_specs=[row(RWKV_WIDTH)] * 8,
        out_shape=[jax.ShapeDtypeStruct((n, RWKV_WIDTH), F32)] * 8,
        compiler_params=_params("parallel"),
        name="rwkv_prep",
    )(cols, shifted, *wts)


def _sublane_allsum(x):
    x = x + pltpu.roll(x, 4, 0)
    x = x + pltpu.roll(x, 2, 0)
    return x + pltpu.roll(x, 1, 0)


def _scan_kernel(w_ref, nkk_ref, kka_ref, k_ref, r_ref, v_ref, s0_ref, y_ref, s_ref, *, n_steps, n_ih, unroll):
    @pl.when(pl.program_id(0) == 0)
    def _():
        s_ref[...] = s0_ref[...]

    nv = RWKV_HEAD // SUBLANES
    shape3 = (nv, SUBLANES, LANES)

    def step(t, carry):
        def group(gi, c2):
            for u in range(unroll):
                ih = gi * unroll + u
                s = s_ref[ih].reshape(shape3)
                sa = _sublane_allsum(jnp.sum(s * nkk_ref[t].reshape(shape3), axis=0))
                v_row = v_ref[t, pl.ds(ih, 1), :]
                s = (s * w_ref[t].reshape(shape3) + kka_ref[t].reshape(shape3) * sa[None]
                     + k_ref[t].reshape(shape3) * v_row[None])
                s_ref[ih] = s.reshape(RWKV_HEAD, LANES)
                y = _sublane_allsum(jnp.sum(s * r_ref[t].reshape(shape3), axis=0))
                y_ref[t, pl.ds(ih, 1), :] = y[0:1, :]
            return c2

        lax.fori_loop(0, n_ih // unroll, group, 0)
        return carry

    lax.fori_loop(0, n_steps, step, 0)


def _scan(ops, v_rows, s0):
    T = v_rows.shape[0]
    n_ih = s0.shape[0]
    tc = _tile(T, SCAN_STEPS, 1)
    unroll = _tile(n_ih, 16, 1)
    key_spec = pl.BlockSpec((tc, RWKV_HEAD, LANES), lambda i: (i, 0, 0))
    val_spec = pl.BlockSpec((tc, n_ih, LANES), lambda i: (i, 0, 0))
    return pl.pallas_call(
        functools.partial(_scan_kernel, n_steps=tc, n_ih=n_ih, unroll=unroll),
        grid=(T // tc,),
        in_specs=[key_spec] * 5 + [val_spec, _full(s0.shape)],
        out_specs=[val_spec, _full(s0.shape)],
        out_shape=[jax.ShapeDtypeStruct((T, n_ih, LANES), F32), jax.ShapeDtypeStruct(s0.shape, F32)],
        compiler_params=_params("arbitrary"),
        name="rwkv_scan",
    )(*ops, v_rows, s0)


def _to_key_tiles(x, B, T, i_lo):
    x = x.reshape(B, T, RWKV_HEADS, RWKV_HEAD).transpose(1, 3, 0, 2).reshape(T, RWKV_HEAD, B * RWKV_HEADS)
    return jnp.tile(x, (1, 1, i_lo))


def _to_val_rows(x, B, T, i_lo):
    n_ih = RWKV_HEAD // i_lo
    x = x.reshape(B, T, RWKV_HEADS, n_ih, i_lo).transpose(1, 3, 4, 0, 2)
    return x.reshape(T, n_ih, i_lo * B * RWKV_HEADS)


def _from_val_rows(y, B, T, i_lo):
    n_ih = RWKV_HEAD // i_lo
    y = y.reshape(T, n_ih, i_lo, B, RWKV_HEADS).transpose(3, 0, 4, 1, 2)
    return y.reshape(B * T, RWKV_WIDTH)


def _state_to_tiles(s, i_lo):
    B = s.shape[0]
    n_ih = RWKV_HEAD // i_lo
    s = s.reshape(B, RWKV_HEADS, n_ih, i_lo, RWKV_HEAD).transpose(2, 4, 3, 0, 1)
    return s.reshape(n_ih, RWKV_HEAD, i_lo * B * RWKV_HEADS)


def _state_from_tiles(s, B, i_lo):
    n_ih = RWKV_HEAD // i_lo
    s = s.reshape(n_ih, RWKV_HEAD, i_lo, B, RWKV_HEADS).transpose(3, 4, 0, 2, 1)
    return s.reshape(B, RWKV_HEADS, RWKV_HEAD, RWKV_HEAD)


def _outproj_kernel(x_ref, at_ref, cv_ref, y_ref, bonus_ref, g_ref, lg_ref, lb_ref, ones_ref,
                    wa_ref, wc_ref, wr_ref, n1g_ref, n1b_ref, out_ref, *, alpha):
    ones_bd = ones_ref[...]
    y = y_ref[...]
    inv = 1.0 / RWKV_HEAD
    mu = _segsum(y, ones_bd) * inv
    d = y - mu
    var = _segsum(d * d, ones_bd) * inv
    yn = d * lax.rsqrt(var + LNX_EPS) * lg_ref[...] + lb_ref[...]
    rw = ((yn + bonus_ref[...]) * g_ref[...]).astype(BF16)
    h = _dot(at_ref[...], wa_ref[...]) + _dot(cv_ref[...].astype(BF16), wc_ref[...]) + _dot(rw, wr_ref[...])
    out_ref[...] = _layer_norm_rows(alpha * x_ref[...] + h, n1g_ref[...], n1b_ref[...])


def _outproj(x, attn, conv, y, bonus, g, w, alpha):
    n = x.shape[0]
    tm = _tile(n, ROW_TILE)
    row = lambda c: pl.BlockSpec((tm, c), lambda i: (i, 0))
    wts = [w['rwkv_lnx_g'], w['rwkv_lnx_b'], w['ones_bd'], w['wo_a'], w['wo_c'], w['wo_r'], w['ln1_g'], w['ln1_b']]
    return pl.pallas_call(
        functools.partial(_outproj_kernel, alpha=alpha),
        grid=(n // tm,),
        in_specs=[row(D_MODEL), row(MLA_WIDTH), row(CONV_CH), row(RWKV_WIDTH), row(RWKV_WIDTH), row(RWKV_WIDTH)]
                 + [_full(a.shape) for a in wts],
        out_specs=row(D_MODEL),
        out_shape=jax.ShapeDtypeStruct((n, D_MODEL), F32),
        compiler_params=_params("parallel"),
        name="out_proj",
    )(x, attn, conv, y, bonus, g, *wts)


def _norm_ple(x1, f, p, n2g, n2b, wple, wgate, alpha):
    x2 = _layer_norm_rows(alpha * x1 + f, n2g, n2b)
    gate = _sigmoid(_dot(x2.astype(BF16), wgate))
    return x2 + gate * _dot(p.astype(BF16), wple)


def _ffn_kernel(x_ref, p_ref, wg_ref, wu_ref, wd_ref, n2g_ref, n2b_ref, wple_ref, wgate_ref,
                out_ref, xb_ref, acc_ref, *, alpha):
    f = pl.program_id(1)

    @pl.when(f == 0)
    def _():
        xb_ref[...] = x_ref[...].astype(BF16)
        acc_ref[...] = jnp.zeros(acc_ref.shape, F32)

    xb = xb_ref[...]
    gate = _dot(xb, wg_ref[...])
    hid = (gate * _sigmoid(gate) * _dot(xb, wu_ref[...])).astype(BF16)
    acc_ref[...] += _dot(hid, wd_ref[...])

    @pl.when(f == pl.num_programs(1) - 1)
    def _():
        out_ref[...] = _norm_ple(x_ref[...], acc_ref[...], p_ref[...], n2g_ref[...], n2b_ref[...],
                                 wple_ref[...], wgate_ref[...], alpha)


def _ffn(x, p, wg, wu, wd, w, alpha):
    n = x.shape[0]
    d_ff = wg.shape[1]
    tm = _tile(n, ROW_TILE)
    tf = _tile(d_ff, 1408, LANES)
    row = lambda c: pl.BlockSpec((tm, c), lambda i, f: (i, 0))
    wts = [w['ln2_g'], w['ln2_b'], w['ple_w'], w['ple_gate_w']]
    return pl.pallas_call(
        functools.partial(_ffn_kernel, alpha=alpha),
        grid=(n // tm, d_ff // tf),
        in_specs=[row(D_MODEL), row(p.shape[1]),
                  pl.BlockSpec((D_MODEL, tf), lambda i, f: (0, f)),
                  pl.BlockSpec((D_MODEL, tf), lambda i, f: (0, f)),
                  pl.BlockSpec((tf, D_MODEL), lambda i, f: (f, 0))] + [_full(a.shape) for a in wts],
        out_specs=row(D_MODEL),
        out_shape=jax.ShapeDtypeStruct((n, D_MODEL), F32),
        scratch_shapes=[pltpu.VMEM((tm, D_MODEL), BF16), pltpu.VMEM((tm, D_MODEL), F32)],
        compiler_params=_params("parallel", "arbitrary"),
        name="ffn_dense",
    )(x, p, wg, wu, wd, *wts)


def _router_kernel(x_ref, wr_ref, idx_ref, gate_ref):
    xh, xm, _ = _split3(x_ref[...])
    wh, wm, _ = _split3(wr_ref[...])
    lt = _dot_nt(wh, xh) + _dot_nt(wh, xm) + _dot_nt(wm, xh)
    e = lax.broadcasted_iota(jnp.int32, lt.shape, 0).astype(F32)
    lt = jnp.where(e < N_EXPERTS, lt, -jnp.inf)
    m1 = jnp.max(lt, axis=0, keepdims=True)
    i1 = jnp.min(jnp.where(lt == m1, e, float(N_EXPERTS)), axis=0, keepdims=True)
    lt2 = jnp.where(e == i1, -jnp.inf, lt)
    m2 = jnp.max(lt2, axis=0, keepdims=True)
    i2 = jnp.min(jnp.where(lt2 == m2, e, float(N_EXPERTS)), axis=0, keepdims=True)
    d = jnp.exp(m2 - m1)
    idx_ref[0:1, :] = i1.astype(jnp.int32)
    idx_ref[1:2, :] = i2.astype(jnp.int32)
    gate_ref[0:1, :] = 1.0 / (1.0 + d)
    gate_ref[1:2, :] = d / (1.0 + d)


def _router(x, wr_t):
    n = x.shape[0]
    tm = _tile(n, ROW_TILE, LANES)
    return pl.pallas_call(
        _router_kernel,
        grid=(n // tm,),
        in_specs=[pl.BlockSpec((tm, D_MODEL), lambda i: (i, 0)), _full(wr_t.shape)],
        out_specs=[pl.BlockSpec((TOP_K, tm), lambda i: (0, i)), pl.BlockSpec((TOP_K, tm), lambda i: (0, i))],
        out_shape=[jax.ShapeDtypeStruct((TOP_K, n), jnp.int32), jax.ShapeDtypeStruct((TOP_K, n), F32)],
        compiler_params=_params("parallel"),
        name="moe_router",
    )(x, wr_t)


def _row_copy(src_hbm, buf_ref, sem, r, src_row):
    return pltpu.make_async_copy(src_hbm.at[pl.ds(src_row, 1)], buf_ref.at[pl.ds(r, 1)], sem)


def _gather_start(idx_ref, base, src_hbm, buf_ref, sem, n_rows):
    def start(r, c):
        _row_copy(src_hbm, buf_ref, sem, r, idx_ref[base + r]).start()
        return c

    lax.fori_loop(0, n_rows, start, 0, unroll=GATHER_UNROLL)


def _gather_wait(src_hbm, buf_ref, sem, n_rows):
    def wait(r, c):
        _row_copy(src_hbm, buf_ref, sem, r, 0).wait()
        return c

    lax.fori_loop(0, n_rows, wait, 0, unroll=GATHER_UNROLL)


def _moe_gather_kernel(tok_ref, used_ref, x_hbm, out_ref, buf_ref, sem, *, rows):
    base = pl.program_id(0) * rows

    @pl.when(base < used_ref[0])
    def _():
        _gather_start(tok_ref, base, x_hbm, buf_ref, sem, rows)
        _gather_wait(x_hbm, buf_ref, sem, rows)
        out_ref[...] = buf_ref[...].astype(BF16)

    @pl.when(base >= used_ref[0])
    def _():
        out_ref[...] = jnp.zeros(out_ref.shape, BF16)


def _moe_gather(row_tok, used_rows, x):
    n_rows = row_tok.shape[0]
    rows = _tile(n_rows, GATHER_ROWS, 16)
    grid_spec = pltpu.PrefetchScalarGridSpec(
        num_scalar_prefetch=2,
        grid=(n_rows // rows,),
        in_specs=[pl.BlockSpec(memory_space=pl.ANY)],
        out_specs=pl.BlockSpec((rows, D_MODEL), lambda i, tok, used: (i, 0)),
        scratch_shapes=[pltpu.VMEM((rows, D_MODEL), F32), pltpu.SemaphoreType.DMA],
    )
    return pl.pallas_call(
        functools.partial(_moe_gather_kernel, rows=rows),
        grid_spec=grid_spec,
        out_shape=jax.ShapeDtypeStruct((n_rows, D_MODEL), BF16),
        compiler_params=_params("arbitrary"),
        name="moe_gather",
    )(row_tok, used_rows, x)


def _moe_ffn_kernel(be_ref, used_ref, x_ref, wg_ref, wu_ref, wd_ref, out_ref, acc_ref):
    del be_ref
    i = pl.program_id(0)
    f = pl.program_id(1)
    last_f = pl.num_programs(1) - 1

    @pl.when(i < used_ref[0])
    def _():
        @pl.when(f == 0)
        def _():
            acc_ref[...] = jnp.zeros(acc_ref.shape, F32)

        xb = x_ref[...]
        gate = _dot(xb, wg_ref[0, 0])
        hid = (gate * _sigmoid(gate) * _dot(xb, wu_ref[0, 0])).astype(BF16)
        acc_ref[...] += _dot(hid, wd_ref[0, 0])

        @pl.when(f == last_f)
        def _():
            out_ref[...] = acc_ref[...]

    @pl.when((i >= used_ref[0]) & (f == last_f))
    def _():
        out_ref[...] = jnp.zeros(out_ref.shape, F32)


def _moe_ffn(block_e, used_blocks, xs, wg, wu, wd, j, m_rows):
    n_rows = xs.shape[0]
    d_ff = wg.shape[3]
    tf = _tile(d_ff, 896, LANES)
    n_f = d_ff // tf

    def blk(i, used):
        return jnp.minimum(i, used[0] - 1)

    def ftile(i, f, used):
        return jnp.where(i < used[0], f, n_f - 1)

    grid_spec = pltpu.PrefetchScalarGridSpec(
        num_scalar_prefetch=2,
        grid=(n_rows // m_rows, n_f),
        in_specs=[pl.BlockSpec((m_rows, D_MODEL), lambda i, f, be, used: (blk(i, used), 0)),
                  pl.BlockSpec((1, 1, D_MODEL, tf), lambda i, f, be, used: (j, be[blk(i, used)], 0, ftile(i, f, used))),
                  pl.BlockSpec((1, 1, D_MODEL, tf), lambda i, f, be, used: (j, be[blk(i, used)], 0, ftile(i, f, used))),
                  pl.BlockSpec((1, 1, tf, D_MODEL), lambda i, f, be, used: (j, be[blk(i, used)], ftile(i, f, used), 0))],
        out_specs=pl.BlockSpec((m_rows, D_MODEL), lambda i, f, be, used: (i, 0)),
        scratch_shapes=[pltpu.VMEM((m_rows, D_MODEL), F32)],
    )
    return pl.pallas_call(
        _moe_ffn_kernel,
        grid_spec=grid_spec,
        out_shape=jax.ShapeDtypeStruct((n_rows, D_MODEL), F32),
        compiler_params=_params("parallel", "arbitrary"),
        name="moe_ffn",
    )(block_e, used_blocks, xs, wg, wu, wd)


def _moe_combine_kernel(pos0_ref, pos1_ref, rows_hbm, x_ref, p_ref, gate_ref, n2g_ref, n2b_ref, wple_ref, wgate_ref,
                        out_ref, b0_ref, b1_ref, sem0, sem1, *, tm, alpha):
    base = pl.program_id(0) * tm
    _gather_start(pos0_ref, base, rows_hbm, b0_ref, sem0, tm)
    _gather_start(pos1_ref, base, rows_hbm, b1_ref, sem1, tm)
    _gather_wait(rows_hbm, b0_ref, sem0, tm)
    _gather_wait(rows_hbm, b1_ref, sem1, tm)
    gts = gate_ref[...]
    f = b0_ref[...] * gts[:, 0:1] + b1_ref[...] * gts[:, 1:2]
    out_ref[...] = _norm_ple(x_ref[...], f, p_ref[...], n2g_ref[...], n2b_ref[...],
                             wple_ref[...], wgate_ref[...], alpha)


def _moe_combine(pos0, pos1, out_rows, x, p, gates, w, alpha):
    n = x.shape[0]
    tm = _tile(n, GATHER_ROWS)
    row = lambda c: pl.BlockSpec((tm, c), lambda i, a, b: (i, 0))
    wts = [w['ln2_g'], w['ln2_b'], w['ple_w'], w['ple_gate_w']]
    grid_spec = pltpu.PrefetchScalarGridSpec(
        num_scalar_prefetch=2,
        grid=(n // tm,),
        in_specs=[pl.BlockSpec(memory_space=pl.ANY), row(D_MODEL), row(p.shape[1]), row(TOP_K)]
                 + [_full(a.shape) for a in wts],
        out_specs=row(D_MODEL),
        scratch_shapes=[pltpu.VMEM((tm, D_MODEL), F32), pltpu.VMEM((tm, D_MODEL), F32),
                        pltpu.SemaphoreType.DMA, pltpu.SemaphoreType.DMA],
    )
    return pl.pallas_call(
        functools.partial(_moe_combine_kernel, tm=tm, alpha=alpha),
        grid_spec=grid_spec,
        out_shape=jax.ShapeDtypeStruct((n, D_MODEL), F32),
        compiler_params=_params("arbitrary"),
        name="moe_combine",
    )(pos0, pos1, out_rows, x, p, gates, *wts)


def _moe(x, p, wr_t, wg, wu, wd, j, w, alpha):
    n = x.shape[0]
    idx, gates = _router(x, wr_t)
    flat_e = idx.T.reshape(-1)
    onehot = (flat_e[:, None] == jnp.arange(N_EXPERTS, dtype=jnp.int32)[None, :]).astype(jnp.int32)
    csum = jnp.cumsum(onehot, axis=0)
    rank = jnp.take_along_axis(csum, flat_e[:, None], axis=1)[:, 0] - 1
    counts = csum[-1]
    m_rows = min(MOE_ROWS, max(SUBLANES * 2, n))
    padded = (counts + m_rows - 1) // m_rows * m_rows
    pad_end = jnp.cumsum(padded)
    dest = (pad_end - padded)[flat_e] + rank
    n_assign = n * TOP_K
    n_blocks = -(-n_assign // m_rows) + N_EXPERTS
    flat_tok = jnp.repeat(jnp.arange(n, dtype=jnp.int32), TOP_K)
    row_tok = jnp.zeros((n_blocks * m_rows,), jnp.int32).at[dest].set(flat_tok)
    block_e = jnp.minimum(jnp.searchsorted(pad_end, jnp.arange(n_blocks, dtype=jnp.int32) * m_rows, side='right'),
                          N_EXPERTS - 1).astype(jnp.int32)
    used_rows = pad_end[-1:].astype(jnp.int32)
    xs = _moe_gather(row_tok, used_rows, x)
    out_rows = _moe_ffn(block_e, used_rows // m_rows, xs, wg, wu, wd, j, m_rows)
    dest2 = dest.reshape(n, TOP_K)
    return _moe_combine(dest2[:, 0], dest2[:, 1], out_rows, x, p, gates.T, w, alpha)


def _rot_cols(wmat):
    half = MLA_ROPE // 2
    return jnp.concatenate([-wmat[..., half:], wmat[..., :half]], axis=-1)


def _layer_weights(l, a):
    o1 = Q_LORA
    o2 = o1 + KV_LORA
    o3 = o2 + MLA_ROPE
    o4 = o3 + 2 * CONV_CH
    w_in = a['w_in'][l]
    w = {}
    w['wq'] = w_in[:, :o1].astype(BF16)
    w['wkv'] = w_in[:, o1:o2].astype(BF16)
    wkr = w_in[:, o2:o3]
    w['wkr8'] = jnp.tile(wkr, (1, MLA_HEADS)).astype(BF16)
    w['wkrr8'] = jnp.tile(_rot_cols(wkr), (1, MLA_HEADS)).astype(BF16)
    w['wcv'] = w_in[:, o3:o4].astype(BF16)
    w['wrw'] = w_in[:, o4:].astype(BF16)
    w['gq'] = a['q_norm_g'][l][None, :]
    w['gkv'] = a['kv_norm_g'][l][None, :]
    wuq = a['w_uq'][l].reshape(Q_LORA, MLA_HEADS, MLA_NOPE + MLA_ROPE)
    w['wuqn'] = wuq[:, :, :MLA_NOPE].reshape(Q_LORA, MLA_HEADS * MLA_NOPE).astype(BF16)
    wuqr = wuq[:, :, MLA_NOPE:]
    w['wuqr'] = wuqr.reshape(Q_LORA, MLA_HEADS * MLA_ROPE).astype(BF16)
    w['wuqrr'] = _rot_cols(wuqr).reshape(Q_LORA, MLA_HEADS * MLA_ROPE).astype(BF16)
    wuk_t = a['w_uk'][l].transpose(1, 2, 0)
    zeros = jnp.zeros_like(wuk_t)
    even = jnp.concatenate([wuk_t, zeros], axis=1)
    odd = jnp.concatenate([zeros, wuk_t], axis=1)
    is_odd = (jnp.arange(MLA_HEADS) % 2 == 1)[:, None, None]
    w['wukp'] = jnp.where(is_odd, odd, even).astype(BF16)
    wuv = a['w_uv'][l].transpose(1, 0, 2)
    zv = jnp.zeros((MLA_HEADS // 2, KV_LORA, MLA_V), F32)
    top = jnp.concatenate([wuv[0::2], zv], axis=2)
    bot = jnp.concatenate([zv, wuv[1::2]], axis=2)
    w['wuvp'] = jnp.concatenate([top, bot], axis=1).astype(BF16)
    w['mla_out_g'] = a['mla_out_g'][l][None, :]
    w['conv_w'] = a['conv_w'][l]
    for name in ('conv_b', 'conv_ln_g', 'conv_ln_b', 'rwkv_mu', 'rwkv_w0', 'rwkv_a0', 'rwkv_k_k', 'rwkv_k_a',
                 'rwkv_lnx_g', 'rwkv_lnx_b', 'ln1_g', 'ln1_b', 'ln2_g', 'ln2_b'):
        w[name] = a[name][l][None, :]
    w['rwkv_r_k'] = a['rwkv_r_k'][l].reshape(1, RWKV_WIDTH)
    pad = lambda m, lo: jnp.zeros((LANES, RWKV_WIDTH), F32).at[lo:lo + m.shape[0]].set(m).astype(BF16)
    w['w2p'] = pad(a['rwkv_w2'][l], 0)
    w['a2p'] = pad(a['rwkv_a2'][l], W_LORA)
    w['g2p'] = pad(a['rwkv_g2'][l], W_LORA + A_LORA)
    head_of = jnp.arange(RWKV_WIDTH) // RWKV_HEAD
    w['ones_bd'] = (head_of[:, None] == head_of[None, :]).astype(BF16)
    w_out = a['w_out'][l]
    w['wo_a'] = w_out[:MLA_WIDTH].astype(BF16)
    w['wo_c'] = w_out[MLA_WIDTH:MLA_WIDTH + CONV_CH].astype(BF16)
    w['wo_r'] = w_out[MLA_WIDTH + CONV_CH:].astype(BF16)
    w['ple_w'] = a['ple_w'][l].astype(BF16)
    w['ple_gate_w'] = a['ple_gate_w'][l].astype(BF16)
    return w


def _rope_tables(pos):
    half = MLA_ROPE // 2
    inv = ROPE_BASE ** (-jnp.arange(half, dtype=F32) / half)
    ang = pos.astype(F32)[:, None] * inv[None, :]
    cos = jnp.cos(ang)
    sin = jnp.sin(ang)
    rep = lambda t: jnp.tile(jnp.concatenate([t, t], axis=-1), (1, MLA_HEADS))
    return rep(cos), rep(sin)


def _run_group(x, p_emb, pos, a, layer_w, mixers_w, depth, alpha, past):
    B, T, _ = x.shape
    n = B * T
    x = x.reshape(n, D_MODEL)
    cos_t, sin_t = _rope_tables(pos)
    chains = B * RWKV_HEADS
    assert LANES % chains == 0 and RWKV_HEAD % (LANES // chains) == 0
    i_lo = LANES // chains
    news = []
    for l in range(depth):
        w = layer_w[l]
        q, kvb, ckv, kr, u, rcols = _proj(x, cos_t, sin_t, w, T)
        if past is None:
            o = _attn_prompt(q, kvb, B, T)
            conv_buf = jnp.zeros((B, CONV_W - 1, CONV_CH), F32)
            shift_buf = jnp.zeros((B, RWKV_COLS), F32)
            wkv0 = jnp.zeros((B, RWKV_HEADS, RWKV_HEAD, RWKV_HEAD), F32)
        else:
            rows = MLA_HEADS * T
            q4 = q.reshape(MLA_HEADS, B, T, QK_WIDTH).transpose(1, 0, 2, 3)
            ql = q4[..., :KV_LORA].reshape(B, rows, KV_LORA)
            qr = jnp.stack([q4[:, h, :, KV_LORA + h * MLA_ROPE:KV_LORA + (h + 1) * MLA_ROPE]
                            for h in range(MLA_HEADS)], axis=1).reshape(B, rows, MLA_ROPE)
            o = _attn_sample(ql, qr, ckv.reshape(B, T, KV_LORA), kr.reshape(B, T, MLA_ROPE),
                             past['cache_ckv'], past['cache_kr'], past['page_table'], l)
            o = o.reshape(B, MLA_HEADS, T, KV_LORA).transpose(1, 0, 2, 3).reshape(MLA_HEADS, n, KV_LORA).astype(BF16)
            conv_buf, shift_buf, wkv0 = past['state_conv'][l], past['state_shift'][l], past['state_wkv'][l]
        attn = _uvnorm(o, w['wuvp'], w['mla_out_g'])
        conv, conv_new = _conv(u.reshape(B, T, CONV_CH), conv_buf, w['conv_w'], w['conv_b'],
                               w['conv_ln_g'], w['conv_ln_b'])
        rc3 = rcols.reshape(B, T, RWKV_COLS)
        shifted = jnp.concatenate([shift_buf[:, None, :], rc3[:, :-1]], axis=1).reshape(n, RWKV_COLS)
        r, dec, nkk, kka, kp, v, g, bonus = _rwkv_prep(rcols, shifted, w)
        ops = [_to_key_tiles(t, B, T, i_lo) for t in (dec, nkk, kka, kp, r)]
        y_rows, s_fin = _scan(ops, _to_val_rows(v, B, T, i_lo), _state_to_tiles(wkv0, i_lo))
        y = _from_val_rows(y_rows, B, T, i_lo)
        x1 = _outproj(x, attn, conv.reshape(n, CONV_CH), y, bonus, g, w, alpha)
        j = l // 2
        p_l = p_emb[l].reshape(n, -1)
        if l % 2 == 0:
            x = _ffn(x1, p_l, mixers_w['ffn_g'][j], mixers_w['ffn_u'][j], mixers_w['ffn_d'][j], w, alpha)
        else:
            x = _moe(x1, p_l, mixers_w['router_t'][j], mixers_w['moe_g'], mixers_w['moe_u'],
                     mixers_w['moe_d'], j, w, alpha)
        news.append((ckv.reshape(B, T, KV_LORA), kr.reshape(B, T, MLA_ROPE), conv_new, rc3[:, -1],
                     _state_from_tiles(s_fin, B, i_lo)))
    outs = tuple(jnp.stack([st[i] for st in news]) for i in range(5))
    return (x.reshape(B, T, D_MODEL),) + outs


def kernel(x_prompt, x_sample, cache_ckv, cache_kr, state_conv, state_shift, state_wkv, page_table,
           p_prompt, p_sample, w_in, q_norm_g, w_uq, kv_norm_g, w_uk, w_uv, mla_out_g,
           conv_w, conv_b, conv_ln_g, conv_ln_b, rwkv_mu, rwkv_w0, rwkv_w2, rwkv_a0, rwkv_a2,
           rwkv_g2, rwkv_k_k, rwkv_k_a, rwkv_r_k, rwkv_lnx_g, rwkv_lnx_b, w_out, ln1_g, ln1_b,
           ffn_w_gate, ffn_w_up, ffn_w_down, moe_router, moe_w_gate, moe_w_up, moe_w_down,
           ln2_g, ln2_b, ple_w, ple_gate_w):
    a = dict(w_in=w_in, q_norm_g=q_norm_g, w_uq=w_uq, kv_norm_g=kv_norm_g, w_uk=w_uk, w_uv=w_uv,
             mla_out_g=mla_out_g, conv_w=conv_w, conv_b=conv_b, conv_ln_g=conv_ln_g, conv_ln_b=conv_ln_b,
             rwkv_mu=rwkv_mu, rwkv_w0=rwkv_w0, rwkv_w2=rwkv_w2, rwkv_a0=rwkv_a0, rwkv_a2=rwkv_a2,
             rwkv_g2=rwkv_g2, rwkv_k_k=rwkv_k_k, rwkv_k_a=rwkv_k_a, rwkv_r_k=rwkv_r_k,
             rwkv_lnx_g=rwkv_lnx_g, rwkv_lnx_b=rwkv_lnx_b, w_out=w_out, ln1_g=ln1_g, ln1_b=ln1_b,
             ln2_g=ln2_g, ln2_b=ln2_b, ple_w=ple_w, ple_gate_w=ple_gate_w)
    depth = w_in.shape[0]
    alpha = (2 * depth) ** 0.25
    layer_w = [_layer_weights(l, a) for l in range(depth)]
    mixers_w = dict(ffn_g=ffn_w_gate.astype(BF16), ffn_u=ffn_w_up.astype(BF16), ffn_d=ffn_w_down.astype(BF16),
                    router_t=jnp.pad(moe_router.transpose(0, 2, 1), ((0, 0), (0, ROUTER_ROWS - N_EXPERTS), (0, 0))),
                    moe_g=moe_w_gate.astype(BF16), moe_u=moe_w_up.astype(BF16), moe_d=moe_w_down.astype(BF16))
    n_pages = page_table.shape[1]
    past_len = n_pages * PAGE_SIZE
    pos_prompt = jnp.arange(x_prompt.shape[1], dtype=jnp.int32)
    pos_sample = past_len + jnp.arange(x_sample.shape[1], dtype=jnp.int32)
    past = dict(cache_ckv=cache_ckv, cache_kr=cache_kr.transpose(0, 1, 3, 2), page_table=page_table,
                state_conv=state_conv, state_shift=state_shift, state_wkv=state_wkv)
    yp = _run_group(x_prompt, p_prompt, pos_prompt, a, layer_w, mixers_w, depth, alpha, None)
    ys = _run_group(x_sample, p_sample, pos_sample, a, layer_w, mixers_w, depth, alpha, past)
    return (yp[0], ys[0]) + yp[1:] + ys[1:]
```

```python
import functools

import jax
import jax.numpy as jnp
from jax import lax
from jax.experimental import pallas as pl
from jax.experimental.pallas import tpu as pltpu

F32 = jnp.float32
BF16 = jnp.bfloat16

D_MODEL = 1024
MLA_HEADS = 8
MLA_NOPE = 64
MLA_ROPE = 32
MLA_V = 64
MLA_WIDTH = MLA_HEADS * MLA_V
Q_LORA = 384
KV_LORA = 256
ROPE_BASE = 10000.0
MLA_SCALE = (MLA_NOPE + MLA_ROPE) ** -0.5
CONV_CH = 256
CONV_W = 31
RWKV_WIDTH = 256
RWKV_HEAD = 64
RWKV_HEADS = 4
W_LORA = 32
A_LORA = 32
G_LORA = 64
RWKV_COLS = 3 * RWKV_WIDTH + W_LORA + A_LORA + G_LORA
LNX_EPS = 64e-5
N_EXPERTS = 8
TOP_K = 2
PAGE_SIZE = 128
LN_EPS = 1e-5
RMS_EPS = 1e-6

LANES = 128
SUBLANES = 8
VMEM_LIMIT_BYTES = 56 * 1024 * 1024
QK_WIDTH = 2 * KV_LORA

ROW_TILE = 512
ATTN_TQ = 128
ATTN_TK = 512
ATTN_ROW_CHUNK = 64
LOG2E = 1.4426950408889634
Q_SCALE = MLA_SCALE * LOG2E
PAGES_PER_STEP = 16
NEW_ROWS_PAD = 16
SCAN_STEPS = 64
MOE_ROWS = 512
GATHER_ROWS = 256
GATHER_UNROLL = 8
ROUTER_ROWS = 16


def _tile(n, pref, mult=SUBLANES):
    if n <= pref:
        return n
    for t in range(pref, 0, -1):
        if n % t == 0 and t % mult == 0:
            return t
    return n


def _params(*sem):
    return pltpu.CompilerParams(dimension_semantics=sem, vmem_limit_bytes=VMEM_LIMIT_BYTES)


def _dot(a, b):
    return jnp.dot(a, b, preferred_element_type=F32)


def _dot_nt(a, b):
    return lax.dot_general(a, b, (((1,), (1,)), ((), ())), preferred_element_type=F32)


def _split3(x):
    hi = x.astype(BF16)
    r1 = x - hi.astype(F32)
    mid = r1.astype(BF16)
    lo = (r1 - mid.astype(F32)).astype(BF16)
    return hi, mid, lo


def _segsum(x, ones_bd):
    hi, mid, lo = _split3(x)
    return _dot(hi, ones_bd) + _dot(mid, ones_bd) + _dot(lo, ones_bd)


def _layer_norm_rows(z, g, b):
    mu = jnp.mean(z, axis=-1, keepdims=True)
    d = z - mu
    var = jnp.mean(d * d, axis=-1, keepdims=True)
    return d * lax.rsqrt(var + LN_EPS) * g + b


def _rms_norm_rows(z, g):
    return z * lax.rsqrt(jnp.mean(z * z, axis=-1, keepdims=True) + RMS_EPS) * g


def _sigmoid(z):
    return 1.0 / (1.0 + jnp.exp(-z))


def _full(shape):
    nd = len(shape)
    return pl.BlockSpec(shape, lambda *_: (0,) * nd)


def _proj_kernel(x_ref, cos_ref, sin_ref, wq_ref, wkv_ref, wkr_ref, wkrr_ref, wcv_ref, wrw_ref,
                 gq_ref, gkv_ref, wuqn_ref, wuqr_ref, wuqrr_ref, wukp_ref,
                 q_ref, kvb_ref, ckv_ref, kr_ref, u_ref, rc_ref):
    xb = x_ref[...].astype(BF16)
    cos = cos_ref[...]
    sin = sin_ref[...]
    cq = _rms_norm_rows(_dot(xb, wq_ref[...]), gq_ref[...]).astype(BF16)
    qn = _dot(cq, wuqn_ref[...]).astype(BF16)
    qr = (_dot(cq, wuqr_ref[...]) * cos + _dot(cq, wuqrr_ref[...]) * sin) * Q_SCALE
    lane = lax.broadcasted_iota(jnp.int32, qr.shape, 1)
    for h in range(MLA_HEADS):
        slab = qn[:, (h // 2) * LANES:(h // 2 + 1) * LANES]
        q_ref[h, :, 0:KV_LORA] = (_dot(slab, wukp_ref[h]) * Q_SCALE).astype(BF16)
        own = (lane >= h * MLA_ROPE) & (lane < (h + 1) * MLA_ROPE)
        q_ref[h, :, KV_LORA:QK_WIDTH] = jnp.where(own, qr, 0.0).astype(BF16)
    ckv = _rms_norm_rows(_dot(xb, wkv_ref[...]), gkv_ref[...])
    ckv_ref[...] = ckv
    kvb_ref[:, 0:KV_LORA] = ckv.astype(BF16)
    kr8 = _dot(xb, wkr_ref[...]) * cos + _dot(xb, wkrr_ref[...]) * sin
    kvb_ref[:, KV_LORA:QK_WIDTH] = kr8.astype(BF16)
    kr_ref[...] = kr8[:, 0:MLA_ROPE]
    cv = _dot(xb, wcv_ref[...])
    u_ref[...] = cv[:, 0:CONV_CH] * _sigmoid(cv[:, CONV_CH:2 * CONV_CH])
    rc_ref[...] = _dot(xb, wrw_ref[...])


def _proj(x, cos_t, sin_t, w, T):
    n = x.shape[0]
    tm = _tile(n, ROW_TILE)
    if T % tm == 0:
        nt = T // tm
        tab_map = lambda i: (i % nt, 0)
    else:
        assert tm % T == 0
        cos_t = jnp.tile(cos_t, (tm // T, 1))
        sin_t = jnp.tile(sin_t, (tm // T, 1))
        tab_map = lambda i: (0, 0)
    row = lambda c: pl.BlockSpec((tm, c), lambda i: (i, 0))
    wts = [w['wq'], w['wkv'], w['wkr8'], w['wkrr8'], w['wcv'], w['wrw'], w['gq'], w['gkv'],
           w['wuqn'], w['wuqr'], w['wuqrr'], w['wukp']]
    return pl.pallas_call(
        _proj_kernel,
        grid=(n // tm,),
        in_specs=[row(D_MODEL), pl.BlockSpec((tm, KV_LORA), tab_map), pl.BlockSpec((tm, KV_LORA), tab_map)]
                 + [_full(a.shape) for a in wts],
        out_specs=[pl.BlockSpec((MLA_HEADS, tm, QK_WIDTH), lambda i: (0, i, 0)),
                   row(QK_WIDTH), row(KV_LORA), row(MLA_ROPE), row(CONV_CH), row(RWKV_COLS)],
        out_shape=[jax.ShapeDtypeStruct((MLA_HEADS, n, QK_WIDTH), BF16),
                   jax.ShapeDtypeStruct((n, QK_WIDTH), BF16),
                   jax.ShapeDtypeStruct((n, KV_LORA), F32),
                   jax.ShapeDtypeStruct((n, MLA_ROPE), F32),
                   jax.ShapeDtypeStruct((n, CONV_CH), F32),
                   jax.ShapeDtypeStruct((n, RWKV_COLS), F32)],
        compiler_params=_params("parallel"),
        name="proj",
    )(x, cos_t, sin_t, *wts)


def _attn_prompt_kernel(q_ref, kv_ref, o_ref, s_ref, p_ref, m_ref, l_ref, a_ref, acc_ref, *, tq, tk, rc):
    qi = pl.program_id(1)
    rows = MLA_HEADS * tq
    q = q_ref[...].reshape(rows, QK_WIDTH)
    m_ref[...] = jnp.full(m_ref.shape, -jnp.inf, F32)
    l_ref[...] = jnp.zeros(l_ref.shape, F32)
    acc_ref[...] = jnp.zeros(acc_ref.shape, F32)

    half = rows // 2

    def block(k0, width, masked):
        kv = kv_ref[pl.ds(k0, width), :]
        s_ref[0:half, 0:width] = _dot_nt(q[0:half], kv)
        s_ref[half:rows, 0:width] = _dot_nt(q[half:rows], kv)

        def scores(c):
            s = s_ref[pl.ds(c * rc, rc), 0:width]
            if masked:
                q_pos = qi * tq + ((c * rc + lax.broadcasted_iota(jnp.int32, (rc, 1), 0)) & (tq - 1))
                k_pos = k0 + lax.broadcasted_iota(jnp.int32, (1, width), 1)
                s = jnp.where(k_pos <= q_pos, s, -jnp.inf)
            return [s[:, g * LANES:(g + 1) * LANES] for g in range(width // LANES)]

        for c in range(rows // rc):
            sl = pl.ds(c * rc, rc)
            mx = functools.reduce(jnp.maximum, scores(c))
            m_prev = m_ref[sl, :]
            m_new = jnp.maximum(m_prev, jnp.max(mx, axis=-1, keepdims=True))
            a_ref[sl, :] = jnp.exp2(m_prev - m_new)
            m_ref[sl, :] = m_new
        for c in range(rows // rc):
            sl = pl.ds(c * rc, rc)
            m_cur = m_ref[sl, :]
            ps = [jnp.exp2(g - m_cur) for g in scores(c)]
            alpha = a_ref[sl, :]
            l_ref[sl, :] = alpha * l_ref[sl, :] + functools.reduce(jnp.add, ps)
            p_ref[sl, 0:width] = jnp.concatenate(ps, axis=-1).astype(BF16)
            acc_ref[sl, :] = acc_ref[sl, :] * jnp.concatenate([alpha] * (KV_LORA // LANES), axis=-1)
        vals = kv[:, 0:KV_LORA]
        acc_ref[0:half, :] += _dot(p_ref[0:half, 0:width], vals)
        acc_ref[half:rows, :] += _dot(p_ref[half:rows, 0:width], vals)

    q0 = qi * tq
    n_full = q0 // tk

    def full_block(kb, carry):
        block(pl.multiple_of(kb * tk, tk), tk, False)
        return carry

    lax.fori_loop(0, n_full, full_block, 0)
    lead = (q0 - n_full * tk) // tq
    for r in range(tk // tq):
        @pl.when(lead == r)
        def _(r=r):
            block(pl.multiple_of(n_full * tk, tk), (r + 1) * tq, True)

    o = acc_ref[...] / jnp.sum(l_ref[...], axis=-1, keepdims=True)
    o_ref[...] = o.reshape(MLA_HEADS, tq, KV_LORA).astype(BF16)


def _attn_prompt(q, kvb, B, T):
    n = B * T
    tq = _tile(T, ATTN_TQ, 16)
    tk = _tile(T, ATTN_TK, 16)
    assert tq & (tq - 1) == 0 and tk % tq == 0
    nq = T // tq
    rows = MLA_HEADS * tq
    rc = _tile(rows, ATTN_ROW_CHUNK, 16)
    return pl.pallas_call(
        functools.partial(_attn_prompt_kernel, tq=tq, tk=tk, rc=rc),
        grid=(B, nq),
        in_specs=[pl.BlockSpec((MLA_HEADS, tq, QK_WIDTH), lambda b, i: (0, b * nq + i, 0)),
                  pl.BlockSpec((T, QK_WIDTH), lambda b, i: (b, 0))],
        out_specs=pl.BlockSpec((MLA_HEADS, tq, KV_LORA), lambda b, i: (0, b * nq + i, 0)),
        out_shape=jax.ShapeDtypeStruct((MLA_HEADS, n, KV_LORA), BF16),
        scratch_shapes=[pltpu.VMEM((rows, tk), F32), pltpu.VMEM((rows, tk), BF16),
                        pltpu.VMEM((rows, LANES), F32), pltpu.VMEM((rows, LANES), F32),
                        pltpu.VMEM((rows, LANES), F32), pltpu.VMEM((rows, KV_LORA), F32)],
        compiler_params=_params("parallel", "parallel"),
        name="attn_prompt",
    )(q, kvb)


def _attn_sample_kernel(pt_ref, ql_ref, qr_ref, *refs, n_pg, n_new):
    del pt_ref
    c_pages = refs[:n_pg]
    r_pages = refs[n_pg:2 * n_pg]
    cn_ref, rn_ref, o_ref, m_ref, l_ref, acc_ref = refs[2 * n_pg:]
    g = pl.program_id(1)

    @pl.when(g == 0)
    def _():
        m_ref[...] = jnp.full(m_ref.shape, -jnp.inf, F32)
        l_ref[...] = jnp.zeros(l_ref.shape, F32)
        acc_ref[...] = jnp.zeros(acc_ref.shape, F32)

    ql = ql_ref[0]
    qr = qr_ref[0]

    def update(s, vals):
        m_prev = m_ref[...]
        m_new = jnp.maximum(m_prev, jnp.max(s, axis=-1, keepdims=True))
        p = jnp.exp2(s - m_new)
        alpha = jnp.exp2(m_prev - m_new)
        l_ref[...] = alpha * l_ref[...] + jnp.sum(p, axis=-1, keepdims=True)
        acc_ref[...] = alpha * acc_ref[...] + _dot(p.astype(BF16), vals)
        m_ref[...] = m_new

    cs = [c_pages[k][0, 0].astype(BF16) for k in range(n_pg)]
    s = jnp.concatenate([_dot_nt(ql, cs[k]) + _dot(qr, r_pages[k][0, 0].astype(BF16))
                         for k in range(n_pg)], axis=-1)
    update(s, jnp.concatenate(cs, axis=0))

    @pl.when(g == pl.num_programs(1) - 1)
    def _():
        cn = cn_ref[0].astype(BF16)
        s_new = _dot_nt(ql, cn) + _dot_nt(qr, rn_ref[0].astype(BF16))
        t_q = lax.broadcasted_iota(jnp.int32, s_new.shape, 0) & (n_new - 1)
        t_k = lax.broadcasted_iota(jnp.int32, s_new.shape, 1)
        update(jnp.where(t_k <= t_q, s_new, -jnp.inf), cn)
        o_ref[0] = acc_ref[...] / l_ref[...]


def _attn_sample(ql, qr, ckv_new, kr_new, cache_ckv, cache_kr_t, page_table, layer):
    B, rows, _ = ql.shape
    n_new = ckv_new.shape[1]
    assert n_new & (n_new - 1) == 0
    n_pad = max(NEW_ROWS_PAD, n_new)
    ckv_new = jnp.pad(ckv_new, ((0, 0), (0, n_pad - n_new), (0, 0)))
    kr_new = jnp.pad(kr_new, ((0, 0), (0, n_pad - n_new), (0, 0)))
    n_pages = page_table.shape[1]
    n_pg = _tile(n_pages, PAGES_PER_STEP, 1)
    n_g = n_pages // n_pg

    def page_spec(shape, k):
        return pl.BlockSpec((1, 1) + shape, lambda b, g, pt: (layer, pt[b * n_pages + g * n_pg + k], 0, 0))

    grid_spec = pltpu.PrefetchScalarGridSpec(
        num_scalar_prefetch=1,
        grid=(B, n_g),
        in_specs=[pl.BlockSpec((1, rows, KV_LORA), lambda b, g, pt: (b, 0, 0)),
                  pl.BlockSpec((1, rows, MLA_ROPE), lambda b, g, pt: (b, 0, 0))]
                 + [page_spec((PAGE_SIZE, KV_LORA), k) for k in range(n_pg)]
                 + [page_spec((MLA_ROPE, PAGE_SIZE), k) for k in range(n_pg)]
                 + [pl.BlockSpec((1, n_pad, KV_LORA), lambda b, g, pt: (b, 0, 0)),
                    pl.BlockSpec((1, n_pad, MLA_ROPE), lambda b, g, pt: (b, 0, 0))],
        out_specs=pl.BlockSpec((1, rows, KV_LORA), lambda b, g, pt: (b, 0, 0)),
        scratch_shapes=[pltpu.VMEM((rows, 1), F32), pltpu.VMEM((rows, 1), F32),
                        pltpu.VMEM((rows, KV_LORA), F32)],
    )
    return pl.pallas_call(
        functools.partial(_attn_sample_kernel, n_pg=n_pg, n_new=n_new),
        grid_spec=grid_spec,
        out_shape=jax.ShapeDtypeStruct((B, rows, KV_LORA), F32),
        compiler_params=_params("parallel", "arbitrary"),
        name="attn_sample",
    )(page_table.reshape(-1), ql, qr, *([cache_ckv] * n_pg), *([cache_kr_t] * n_pg), ckv_new, kr_new)


def _uvnorm_kernel(o_ref, wuvp_ref, g_ref, out_ref):
    parts = []
    for p in range(MLA_HEADS // 2):
        pair = jnp.concatenate([o_ref[2 * p], o_ref[2 * p + 1]], axis=-1)
        parts.append(_dot(pair, wuvp_ref[p]))
    a = jnp.concatenate(parts, axis=-1)
    out_ref[...] = _rms_norm_rows(a, g_ref[...]).astype(BF16)


def _uvnorm(o, wuvp, g):
    n = o.shape[1]
    tm = _tile(n, ROW_TILE)
    return pl.pallas_call(
        _uvnorm_kernel,
        grid=(n // tm,),
        in_specs=[pl.BlockSpec((MLA_HEADS, tm, KV_LORA), lambda i: (0, i, 0)), _full(wuvp.shape), _full(g.shape)],
        out_specs=pl.BlockSpec((tm, MLA_WIDTH), lambda i: (i, 0)),
        out_shape=jax.ShapeDtypeStruct((n, MLA_WIDTH), BF16),
        compiler_params=_params("parallel"),
        name="uv_norm",
    )(o, wuvp, g)


CONV_PAD = 32


def _conv_kernel(u_ref, buf_ref, w_ref, b_ref, g_ref, bb_ref, y_ref, new_ref, up_ref, *, T, tt):
    lead = CONV_PAD - (CONV_W - 1)
    up_ref[lead:CONV_PAD, :] = buf_ref[0]
    up_ref[CONV_PAD:CONV_PAD + T, :] = u_ref[0]
    for t0 in range(0, T, tt):
        acc = jnp.zeros((tt, CONV_CH), F32) + b_ref[...]
        for k in range(CONV_W):
            acc = acc + up_ref[lead + t0 + k:lead + t0 + k + tt, :] * w_ref[k:k + 1, :]
        z = _layer_norm_rows(acc, g_ref[...], bb_ref[...])
        y_ref[0, t0:t0 + tt, :] = z * _sigmoid(z)
    new_ref[0] = up_ref[lead + T:CONV_PAD + T, :]


def _conv(u, buf, w, b, g, bb):
    B, T, _ = u.shape
    tt = _tile(T, 256)
    return pl.pallas_call(
        functools.partial(_conv_kernel, T=T, tt=tt),
        grid=(B,),
        in_specs=[pl.BlockSpec((1, T, CONV_CH), lambda i: (i, 0, 0)),
                  pl.BlockSpec((1, CONV_W - 1, CONV_CH), lambda i: (i, 0, 0)),
                  _full(w.shape), _full(b.shape), _full(g.shape), _full(bb.shape)],
        out_specs=[pl.BlockSpec((1, T, CONV_CH), lambda i: (i, 0, 0)),
                   pl.BlockSpec((1, CONV_W - 1, CONV_CH), lambda i: (i, 0, 0))],
        out_shape=[jax.ShapeDtypeStruct((B, T, CONV_CH), F32),
                   jax.ShapeDtypeStruct((B, CONV_W - 1, CONV_CH), F32)],
        scratch_shapes=[pltpu.VMEM((CONV_PAD + T, CONV_CH), F32)],
        compiler_params=_params("parallel"),
        name="conv_module",
    )(u, buf, w, b, g, bb)


def _rwkv_prep_kernel(c_ref, s_ref, mu_ref, w0_ref, a0_ref, kk_ref, ka_ref, rk_ref,
                      w2_ref, a2_ref, g2_ref, ones_ref,
                      r_out, w_out, nkk_out, kka_out, k_out, v_out, g_out, bonus_out, *, feature_major):
    def put(ref, val):
        ref[...] = val.T if feature_major else val

    cols = c_ref[...]
    xm = cols + (s_ref[...] - cols) * mu_ref[...]
    W = RWKV_WIDTH
    r = xm[:, 0:W]
    k = xm[:, W:2 * W]
    v = xm[:, 2 * W:3 * W]
    tail = xm[:, 3 * W:3 * W + LANES]
    ones_bd = ones_ref[...]
    z = w0_ref[...] + _dot(jnp.tanh(tail).astype(BF16), w2_ref[...])
    softplus = jnp.maximum(-z, 0.0) + jnp.log(1.0 + jnp.exp(-jnp.abs(z)))
    decay = jnp.exp(-jnp.exp(-softplus - 0.5))
    a = _sigmoid(a0_ref[...] + _dot(tail.astype(BF16), a2_ref[...]))
    g = _dot(_sigmoid(tail).astype(BF16), g2_ref[...])
    kk = k * kk_ref[...]
    norm = jnp.sqrt(_segsum(kk * kk, ones_bd))
    kk = kk / jnp.maximum(norm, 1e-12)
    kp = k * (1.0 + (a - 1.0) * ka_ref[...])
    put(r_out, r)
    put(w_out, decay)
    put(nkk_out, -kk)
    put(kka_out, kk * a)
    put(k_out, kp)
    put(v_out, v)
    g_out[...] = g
    bonus_out[...] = _segsum(r * kp * rk_ref[...], ones_bd) * v


def _rwkv_prep(cols, shifted, w, B, T):
    n = cols.shape[0]
    tm = _tile(n, ROW_TILE)
    feature_major = T % tm == 0 and tm % LANES == 0
    row = lambda c: pl.BlockSpec((tm, c), lambda i: (i, 0))
    wts = [w['rwkv_mu'], w['rwkv_w0'], w['rwkv_a0'], w['rwkv_k_k'], w['rwkv_k_a'], w['rwkv_r_k'],
           w['w2p'], w['a2p'], w['g2p'], w['ones_bd']]
    if feature_major:
        nt = T // tm
        op_spec = pl.BlockSpec((RWKV_WIDTH, tm), lambda i: (i // nt, i % nt))
        op_shape = jax.ShapeDtypeStruct((B * RWKV_WIDTH, T), F32)
    else:
        op_spec = row(RWKV_WIDTH)
        op_shape = jax.ShapeDtypeStruct((n, RWKV_WIDTH), F32)
    outs = pl.pallas_call(
        functools.partial(_rwkv_prep_kernel, feature_major=feature_major),
        grid=(n // tm,),
        in_specs=[row(RWKV_COLS), row(RWKV_COLS)] + [_full(a.shape) for a in wts],
        out_specs=[op_spec] * 6 + [row(RWKV_WIDTH)] * 2,
        out_shape=[op_shape] * 6 + [jax.ShapeDtypeStruct((n, RWKV_WIDTH), F32)] * 2,
        compiler_params=_params("parallel"),
        name="rwkv_prep",
    )(cols, shifted, *wts)
    return outs, feature_major


def _sublane_allsum(x):
    x = x + pltpu.roll(x, 4, 0)
    x = x + pltpu.roll(x, 2, 0)
    return x + pltpu.roll(x, 1, 0)


def _scan_kernel(w_ref, nkk_ref, kka_ref, k_ref, r_ref, v_ref, s0_ref, y_ref, s_ref, *, n_steps, n_ih, unroll):
    @pl.when(pl.program_id(0) == 0)
    def _():
        s_ref[...] = s0_ref[...]

    nv = RWKV_HEAD // SUBLANES
    shape3 = (nv, SUBLANES, LANES)

    def key_tile(ref, t):
        return ref[t].reshape(shape3)

    def step(t, carry):
        def group(gi, c2):
            for u in range(unroll):
                ih = gi * unroll + u
                s = s_ref[ih].reshape(shape3)
                sa = _sublane_allsum(jnp.sum(s * key_tile(nkk_ref, t), axis=0))
                v_row = v_ref[ih, pl.ds(t, 1), :]
                s = s * key_tile(w_ref, t) + key_tile(kka_ref, t) * sa[None] + key_tile(k_ref, t) * v_row[None]
                s_ref[ih] = s.reshape(RWKV_HEAD, LANES)
                y = _sublane_allsum(jnp.sum(s * key_tile(r_ref, t), axis=0))
                y_ref[ih, pl.ds(t, 1), :] = y[0:1, :]
            return c2

        lax.fori_loop(0, n_ih // unroll, group, 0)
        return carry

    lax.fori_loop(0, n_steps, step, 0)


def _scan(ops, v_rows, s0):
    n_ih, T, _ = v_rows.shape
    tc = _tile(T, SCAN_STEPS, SUBLANES)
    unroll = _tile(n_ih, 16, 1)
    key_spec = pl.BlockSpec((tc, RWKV_HEAD, LANES), lambda i: (i, 0, 0))
    val_spec = pl.BlockSpec((n_ih, tc, LANES), lambda i: (0, i, 0))
    return pl.pallas_call(
        functools.partial(_scan_kernel, n_steps=tc, n_ih=n_ih, unroll=unroll),
        grid=(T // tc,),
        in_specs=[key_spec] * 5 + [val_spec, _full(s0.shape)],
        out_specs=[val_spec, _full(s0.shape)],
        out_shape=[jax.ShapeDtypeStruct((n_ih, T, LANES), F32), jax.ShapeDtypeStruct(s0.shape, F32)],
        compiler_params=_params("arbitrary"),
        name="rwkv_scan",
    )(*ops, v_rows, s0)


def _scan_in_kernel(src_ref, dst_ref, *, chains, i_lo, key_operand):
    def rows_of(idx):
        return src_ref[pl.ds(idx, chains, stride=RWKV_HEAD), :]

    if key_operand:
        for j in range(RWKV_HEAD):
            dst_ref[:, j, :] = jnp.concatenate([rows_of(j)] * i_lo, axis=0).T
    else:
        for ih in range(RWKV_HEAD // i_lo):
            dst_ref[ih] = jnp.concatenate([rows_of(ih * i_lo + q) for q in range(i_lo)], axis=0).T


def _scan_in(x, B, T, i_lo, key_operand):
    chains = B * RWKV_HEADS
    n_ih = RWKV_HEAD // i_lo
    tt = LANES
    if key_operand:
        out_spec = pl.BlockSpec((tt, RWKV_HEAD, LANES), lambda i: (i, 0, 0))
        out_shape = jax.ShapeDtypeStruct((T, RWKV_HEAD, LANES), F32)
    else:
        out_spec = pl.BlockSpec((n_ih, tt, LANES), lambda i: (0, i, 0))
        out_shape = jax.ShapeDtypeStruct((n_ih, T, LANES), F32)
    return pl.pallas_call(
        functools.partial(_scan_in_kernel, chains=chains, i_lo=i_lo, key_operand=key_operand),
        grid=(T // tt,),
        in_specs=[pl.BlockSpec((chains * RWKV_HEAD, tt), lambda i: (0, i))],
        out_specs=out_spec,
        out_shape=out_shape,
        compiler_params=_params("parallel"),
        name="rwkv_scan_in",
    )(x)


def _scan_out_kernel(y_ref, out_ref, *, chains, i_lo):
    for ih in range(RWKV_HEAD // i_lo):
        tile = y_ref[ih].T
        for q in range(i_lo):
            out_ref[pl.ds(ih * i_lo + q, chains, stride=RWKV_HEAD), :] = tile[q * chains:(q + 1) * chains, :]


def _scan_out(y_rows, B, T, i_lo):
    chains = B * RWKV_HEADS
    n_ih = RWKV_HEAD // i_lo
    tt = LANES
    return pl.pallas_call(
        functools.partial(_scan_out_kernel, chains=chains, i_lo=i_lo),
        grid=(T // tt,),
        in_specs=[pl.BlockSpec((n_ih, tt, LANES), lambda i: (0, i, 0))],
        out_specs=pl.BlockSpec((chains * RWKV_HEAD, tt), lambda i: (0, i)),
        out_shape=jax.ShapeDtypeStruct((chains * RWKV_HEAD, T), F32),
        compiler_params=_params("parallel"),
        name="rwkv_scan_out",
    )(y_rows)


def _to_key_tiles(x, B, T, i_lo):
    x = x.reshape(B, T, RWKV_HEADS, RWKV_HEAD).transpose(1, 3, 0, 2).reshape(T, RWKV_HEAD, B * RWKV_HEADS)
    return jnp.tile(x, (1, 1, i_lo))


def _to_val_rows(x, B, T, i_lo):
    n_ih = RWKV_HEAD // i_lo
    x = x.reshape(B, T, RWKV_HEADS, n_ih, i_lo).transpose(3, 1, 4, 0, 2)
    return x.reshape(n_ih, T, i_lo * B * RWKV_HEADS)


def _from_val_rows(y, B, T, i_lo):
    n_ih = RWKV_HEAD // i_lo
    y = y.reshape(n_ih, T, i_lo, B, RWKV_HEADS).transpose(3, 1, 4, 0, 2)
    return y.reshape(B * T, RWKV_WIDTH)


def _state_to_tiles(s, i_lo):
    B = s.shape[0]
    n_ih = RWKV_HEAD // i_lo
    s = s.reshape(B, RWKV_HEADS, n_ih, i_lo, RWKV_HEAD).transpose(2, 4, 3, 0, 1)
    return s.reshape(n_ih, RWKV_HEAD, i_lo * B * RWKV_HEADS)


def _state_from_tiles(s, B, i_lo):
    n_ih = RWKV_HEAD // i_lo
    s = s.reshape(n_ih, RWKV_HEAD, i_lo, B, RWKV_HEADS).transpose(3, 4, 0, 2, 1)
    return s.reshape(B, RWKV_HEADS, RWKV_HEAD, RWKV_HEAD)


def _outproj_kernel(x_ref, at_ref, cv_ref, y_ref, bonus_ref, g_ref, lg_ref, lb_ref, ones_ref,
                    wa_ref, wc_ref, wr_ref, n1g_ref, n1b_ref, out_ref, *, alpha, y_feature_major):
    ones_bd = ones_ref[...]
    y = y_ref[...].T if y_feature_major else y_ref[...]
    inv = 1.0 / RWKV_HEAD
    mu = _segsum(y, ones_bd) * inv
    d = y - mu
    var = _segsum(d * d, ones_bd) * inv
    yn = d * lax.rsqrt(var + LNX_EPS) * lg_ref[...] + lb_ref[...]
    rw = ((yn + bonus_ref[...]) * g_ref[...]).astype(BF16)
    h = _dot(at_ref[...], wa_ref[...]) + _dot(cv_ref[...].astype(BF16), wc_ref[...]) + _dot(rw, wr_ref[...])
    out_ref[...] = _layer_norm_rows(alpha * x_ref[...] + h, n1g_ref[...], n1b_ref[...])


def _outproj(x, attn, conv, y, bonus, g, w, alpha, T, y_feature_major):
    n = x.shape[0]
    tm = _tile(n, ROW_TILE)
    row = lambda c: pl.BlockSpec((tm, c), lambda i: (i, 0))
    wts = [w['rwkv_lnx_g'], w['rwkv_lnx_b'], w['ones_bd'], w['wo_a'], w['wo_c'], w['wo_r'], w['ln1_g'], w['ln1_b']]
    if y_feature_major:
        nt = T // tm
        y_spec = pl.BlockSpec((RWKV_WIDTH, tm), lambda i: (i // nt, i % nt))
    else:
        y_spec = row(RWKV_WIDTH)
    return pl.pallas_call(
        functools.partial(_outproj_kernel, alpha=alpha, y_feature_major=y_feature_major),
        grid=(n // tm,),
        in_specs=[row(D_MODEL), row(MLA_WIDTH), row(CONV_CH), y_spec, row(RWKV_WIDTH), row(RWKV_WIDTH)]
                 + [_full(a.shape) for a in wts],
        out_specs=row(D_MODEL),
        out_shape=jax.ShapeDtypeStruct((n, D_MODEL), F32),
        compiler_params=_params("parallel"),
        name="out_proj",
    )(x, attn, conv, y, bonus, g, *wts)


def _norm_ple(x1, f, p, n2g, n2b, wple, wgate, alpha):
    x2 = _layer_norm_rows(alpha * x1 + f, n2g, n2b)
    gate = _sigmoid(_dot(x2.astype(BF16), wgate))
    return x2 + gate * _dot(p.astype(BF16), wple)


def _ffn_kernel(x_ref, p_ref, wg_ref, wu_ref, wd_ref, n2g_ref, n2b_ref, wple_ref, wgate_ref,
                out_ref, xb_ref, acc_ref, *, alpha):
    f = pl.program_id(1)

    @pl.when(f == 0)
    def _():
        xb_ref[...] = x_ref[...].astype(BF16)
        acc_ref[...] = jnp.zeros(acc_ref.shape, F32)

    xb = xb_ref[...]
    gate = _dot(xb, wg_ref[...])
    hid = (gate * _sigmoid(gate) * _dot(xb, wu_ref[...])).astype(BF16)
    acc_ref[...] += _dot(hid, wd_ref[...])

    @pl.when(f == pl.num_programs(1) - 1)
    def _():
        out_ref[...] = _norm_ple(x_ref[...], acc_ref[...], p_ref[...], n2g_ref[...], n2b_ref[...],
                                 wple_ref[...], wgate_ref[...], alpha)


def _ffn(x, p, wg, wu, wd, w, alpha):
    n = x.shape[0]
    d_ff = wg.shape[1]
    tm = _tile(n, ROW_TILE)
    tf = _tile(d_ff, 1408, LANES)
    row = lambda c: pl.BlockSpec((tm, c), lambda i, f: (i, 0))
    wts = [w['ln2_g'], w['ln2_b'], w['ple_w'], w['ple_gate_w']]
    return pl.pallas_call(
        functools.partial(_ffn_kernel, alpha=alpha),
        grid=(n // tm, d_ff // tf),
        in_specs=[row(D_MODEL), row(p.shape[1]),
                  pl.BlockSpec((D_MODEL, tf), lambda i, f: (0, f)),
                  pl.BlockSpec((D_MODEL, tf), lambda i, f: (0, f)),
                  pl.BlockSpec((tf, D_MODEL), lambda i, f: (f, 0))] + [_full(a.shape) for a in wts],
        out_specs=row(D_MODEL),
        out_shape=jax.ShapeDtypeStruct((n, D_MODEL), F32),
        scratch_shapes=[pltpu.VMEM((tm, D_MODEL), BF16), pltpu.VMEM((tm, D_MODEL), F32)],
        compiler_params=_params("parallel", "arbitrary"),
        name="ffn_dense",
    )(x, p, wg, wu, wd, *wts)


def _router_kernel(x_ref, wr_ref, idx_ref, gate_ref):
    xh, xm, _ = _split3(x_ref[...])
    wh, wm, _ = _split3(wr_ref[...])
    lt = _dot_nt(wh, xh) + _dot_nt(wh, xm) + _dot_nt(wm, xh)
    e = lax.broadcasted_iota(jnp.int32, lt.shape, 0).astype(F32)
    lt = jnp.where(e < N_EXPERTS, lt, -jnp.inf)
    m1 = jnp.max(lt, axis=0, keepdims=True)
    i1 = jnp.min(jnp.where(lt == m1, e, float(N_EXPERTS)), axis=0, keepdims=True)
    lt2 = jnp.where(e == i1, -jnp.inf, lt)
    m2 = jnp.max(lt2, axis=0, keepdims=True)
    i2 = jnp.min(jnp.where(lt2 == m2, e, float(N_EXPERTS)), axis=0, keepdims=True)
    d = jnp.exp(m2 - m1)
    idx_ref[0:1, :] = i1.astype(jnp.int32)
    idx_ref[1:2, :] = i2.astype(jnp.int32)
    gate_ref[0:1, :] = 1.0 / (1.0 + d)
    gate_ref[1:2, :] = d / (1.0 + d)


def _router(x, wr_t):
    n = x.shape[0]
    tm = _tile(n, ROW_TILE, LANES)
    return pl.pallas_call(
        _router_kernel,
        grid=(n // tm,),
        in_specs=[pl.BlockSpec((tm, D_MODEL), lambda i: (i, 0)), _full(wr_t.shape)],
        out_specs=[pl.BlockSpec((TOP_K, tm), lambda i: (0, i)), pl.BlockSpec((TOP_K, tm), lambda i: (0, i))],
        out_shape=[jax.ShapeDtypeStruct((TOP_K, n), jnp.int32), jax.ShapeDtypeStruct((TOP_K, n), F32)],
        compiler_params=_params("parallel"),
        name="moe_router",
    )(x, wr_t)


def _row_copy(src_hbm, buf_ref, sem, r, src_row):
    return pltpu.make_async_copy(src_hbm.at[pl.ds(src_row, 1)], buf_ref.at[pl.ds(r, 1)], sem)


def _gather_start(idx_ref, base, src_hbm, buf_ref, sem, n_rows):
    def start(r, c):
        _row_copy(src_hbm, buf_ref, sem, r, idx_ref[base + r]).start()
        return c

    lax.fori_loop(0, n_rows, start, 0, unroll=GATHER_UNROLL)


def _gather_wait(src_hbm, buf_ref, sem, n_rows):
    def wait(r, c):
        _row_copy(src_hbm, buf_ref, sem, r, 0).wait()
        return c

    lax.fori_loop(0, n_rows, wait, 0, unroll=GATHER_UNROLL)


def _moe_gather_kernel(tok_ref, used_ref, x_hbm, out_ref, buf_ref, sem, *, rows):
    base = pl.program_id(0) * rows

    @pl.when(base < used_ref[0])
    def _():
        _gather_start(tok_ref, base, x_hbm, buf_ref, sem, rows)
        _gather_wait(x_hbm, buf_ref, sem, rows)
        out_ref[...] = buf_ref[...].astype(BF16)

    @pl.when(base >= used_ref[0])
    def _():
        out_ref[...] = jnp.zeros(out_ref.shape, BF16)


def _moe_gather(row_tok, used_rows, x):
    n_rows = row_tok.shape[0]
    rows = _tile(n_rows, GATHER_ROWS, 16)
    grid_spec = pltpu.PrefetchScalarGridSpec(
        num_scalar_prefetch=2,
        grid=(n_rows // rows,),
        in_specs=[pl.BlockSpec(memory_space=pl.ANY)],
        out_specs=pl.BlockSpec((rows, D_MODEL), lambda i, tok, used: (i, 0)),
        scratch_shapes=[pltpu.VMEM((rows, D_MODEL), F32), pltpu.SemaphoreType.DMA],
    )
    return pl.pallas_call(
        functools.partial(_moe_gather_kernel, rows=rows),
        grid_spec=grid_spec,
        out_shape=jax.ShapeDtypeStruct((n_rows, D_MODEL), BF16),
        compiler_params=_params("arbitrary"),
        name="moe_gather",
    )(row_tok, used_rows, x)


def _moe_ffn_kernel(be_ref, used_ref, x_ref, wg_ref, wu_ref, wd_ref, out_ref, acc_ref):
    del be_ref
    i = pl.program_id(0)
    f = pl.program_id(1)
    last_f = pl.num_programs(1) - 1

    @pl.when(i < used_ref[0])
    def _():
        @pl.when(f == 0)
        def _():
            acc_ref[...] = jnp.zeros(acc_ref.shape, F32)

        xb = x_ref[...]
        gate = _dot(xb, wg_ref[0, 0])
        hid = (gate * _sigmoid(gate) * _dot(xb, wu_ref[0, 0])).astype(BF16)
        acc_ref[...] += _dot(hid, wd_ref[0, 0])

        @pl.when(f == last_f)
        def _():
            out_ref[...] = acc_ref[...]

    @pl.when((i >= used_ref[0]) & (f == last_f))
    def _():
        out_ref[...] = jnp.zeros(out_ref.shape, F32)


def _moe_ffn(block_e, used_blocks, xs, wg, wu, wd, j, m_rows):
    n_rows = xs.shape[0]
    d_ff = wg.shape[3]
    tf = _tile(d_ff, 896, LANES)
    n_f = d_ff // tf

    def blk(i, used):
        return jnp.minimum(i, used[0] - 1)

    def ftile(i, f, used):
        return jnp.where(i < used[0], f, n_f - 1)

    grid_spec = pltpu.PrefetchScalarGridSpec(
        num_scalar_prefetch=2,
        grid=(n_rows // m_rows, n_f),
        in_specs=[pl.BlockSpec((m_rows, D_MODEL), lambda i, f, be, used: (blk(i, used), 0)),
                  pl.BlockSpec((1, 1, D_MODEL, tf), lambda i, f, be, used: (j, be[blk(i, used)], 0, ftile(i, f, used))),
                  pl.BlockSpec((1, 1, D_MODEL, tf), lambda i, f, be, used: (j, be[blk(i, used)], 0, ftile(i, f, used))),
                  pl.BlockSpec((1, 1, tf, D_MODEL), lambda i, f, be, used: (j, be[blk(i, used)], ftile(i, f, used), 0))],
        out_specs=pl.BlockSpec((m_rows, D_MODEL), lambda i, f, be, used: (i, 0)),
        scratch_shapes=[pltpu.VMEM((m_rows, D_MODEL), F32)],
    )
    return pl.pallas_call(
        _moe_ffn_kernel,
        grid_spec=grid_spec,
        out_shape=jax.ShapeDtypeStruct((n_rows, D_MODEL), F32),
        compiler_params=_params("parallel", "arbitrary"),
        name="moe_ffn",
    )(block_e, used_blocks, xs, wg, wu, wd)


def _moe_combine_kernel(pos0_ref, pos1_ref, rows_hbm, x_ref, p_ref, gate_ref, n2g_ref, n2b_ref, wple_ref, wgate_ref,
                        out_ref, b0_ref, b1_ref, sem0, sem1, *, tm, alpha):
    base = pl.program_id(0) * tm
    _gather_start(pos0_ref, base, rows_hbm, b0_ref, sem0, tm)
    _gather_start(pos1_ref, base, rows_hbm, b1_ref, sem1, tm)
    _gather_wait(rows_hbm, b0_ref, sem0, tm)
    _gather_wait(rows_hbm, b1_ref, sem1, tm)
    gts = gate_ref[...]
    f = b0_ref[...] * gts[:, 0:1] + b1_ref[...] * gts[:, 1:2]
    out_ref[...] = _norm_ple(x_ref[...], f, p_ref[...], n2g_ref[...], n2b_ref[...],
                             wple_ref[...], wgate_ref[...], alpha)


def _moe_combine(pos0, pos1, out_rows, x, p, gates, w, alpha):
    n = x.shape[0]
    tm = _tile(n, GATHER_ROWS)
    row = lambda c: pl.BlockSpec((tm, c), lambda i, a, b: (i, 0))
    wts = [w['ln2_g'], w['ln2_b'], w['ple_w'], w['ple_gate_w']]
    grid_spec = pltpu.PrefetchScalarGridSpec(
        num_scalar_prefetch=2,
        grid=(n // tm,),
        in_specs=[pl.BlockSpec(memory_space=pl.ANY), row(D_MODEL), row(p.shape[1]), row(TOP_K)]
                 + [_full(a.shape) for a in wts],
        out_specs=row(D_MODEL),
        scratch_shapes=[pltpu.VMEM((tm, D_MODEL), F32), pltpu.VMEM((tm, D_MODEL), F32),
                        pltpu.SemaphoreType.DMA, pltpu.SemaphoreType.DMA],
    )
    return pl.pallas_call(
        functools.partial(_moe_combine_kernel, tm=tm, alpha=alpha),
        grid_spec=grid_spec,
        out_shape=jax.ShapeDtypeStruct((n, D_MODEL), F32),
        compiler_params=_params("arbitrary"),
        name="moe_combine",
    )(pos0, pos1, out_rows, x, p, gates, *wts)


def _moe(x, p, wr_t, wg, wu, wd, j, w, alpha):
    n = x.shape[0]
    idx, gates = _router(x, wr_t)
    flat_e = idx.T.reshape(-1)
    onehot = (flat_e[:, None] == jnp.arange(N_EXPERTS, dtype=jnp.int32)[None, :]).astype(jnp.int32)
    csum = jnp.cumsum(onehot, axis=0)
    rank = jnp.take_along_axis(csum, flat_e[:, None], axis=1)[:, 0] - 1
    counts = csum[-1]
    m_rows = min(MOE_ROWS, max(SUBLANES * 2, n))
    padded = (counts + m_rows - 1) // m_rows * m_rows
    pad_end = jnp.cumsum(padded)
    dest = (pad_end - padded)[flat_e] + rank
    n_assign = n * TOP_K
    n_blocks = -(-n_assign // m_rows) + N_EXPERTS
    flat_tok = jnp.repeat(jnp.arange(n, dtype=jnp.int32), TOP_K)
    row_tok = jnp.zeros((n_blocks * m_rows,), jnp.int32).at[dest].set(flat_tok)
    block_e = jnp.minimum(jnp.searchsorted(pad_end, jnp.arange(n_blocks, dtype=jnp.int32) * m_rows, side='right'),
                          N_EXPERTS - 1).astype(jnp.int32)
    used_rows = pad_end[-1:].astype(jnp.int32)
    xs = _moe_gather(row_tok, used_rows, x)
    out_rows = _moe_ffn(block_e, used_rows // m_rows, xs, wg, wu, wd, j, m_rows)
    dest2 = dest.reshape(n, TOP_K)
    return _moe_combine(dest2[:, 0], dest2[:, 1], out_rows, x, p, gates.T, w, alpha)


def _rot_cols(wmat):
    half = MLA_ROPE // 2
    return jnp.concatenate([-wmat[..., half:], wmat[..., :half]], axis=-1)


def _layer_weights(l, a):
    o1 = Q_LORA
    o2 = o1 + KV_LORA
    o3 = o2 + MLA_ROPE
    o4 = o3 + 2 * CONV_CH
    w_in = a['w_in'][l]
    w = {}
    w['wq'] = w_in[:, :o1].astype(BF16)
    w['wkv'] = w_in[:, o1:o2].astype(BF16)
    wkr = w_in[:, o2:o3]
    w['wkr8'] = jnp.tile(wkr, (1, MLA_HEADS)).astype(BF16)
    w['wkrr8'] = jnp.tile(_rot_cols(wkr), (1, MLA_HEADS)).astype(BF16)
    w['wcv'] = w_in[:, o3:o4].astype(BF16)
    w['wrw'] = w_in[:, o4:].astype(BF16)
    w['gq'] = a['q_norm_g'][l][None, :]
    w['gkv'] = a['kv_norm_g'][l][None, :]
    wuq = a['w_uq'][l].reshape(Q_LORA, MLA_HEADS, MLA_NOPE + MLA_ROPE)
    w['wuqn'] = wuq[:, :, :MLA_NOPE].reshape(Q_LORA, MLA_HEADS * MLA_NOPE).astype(BF16)
    wuqr = wuq[:, :, MLA_NOPE:]
    w['wuqr'] = wuqr.reshape(Q_LORA, MLA_HEADS * MLA_ROPE).astype(BF16)
    w['wuqrr'] = _rot_cols(wuqr).reshape(Q_LORA, MLA_HEADS * MLA_ROPE).astype(BF16)
    wuk_t = a['w_uk'][l].transpose(1, 2, 0)
    zeros = jnp.zeros_like(wuk_t)
    even = jnp.concatenate([wuk_t, zeros], axis=1)
    odd = jnp.concatenate([zeros, wuk_t], axis=1)
    is_odd = (jnp.arange(MLA_HEADS) % 2 == 1)[:, None, None]
    w['wukp'] = jnp.where(is_odd, odd, even).astype(BF16)
    wuv = a['w_uv'][l].transpose(1, 0, 2)
    zv = jnp.zeros((MLA_HEADS // 2, KV_LORA, MLA_V), F32)
    top = jnp.concatenate([wuv[0::2], zv], axis=2)
    bot = jnp.concatenate([zv, wuv[1::2]], axis=2)
    w['wuvp'] = jnp.concatenate([top, bot], axis=1).astype(BF16)
    w['mla_out_g'] = a['mla_out_g'][l][None, :]
    w['conv_w'] = a['conv_w'][l]
    for name in ('conv_b', 'conv_ln_g', 'conv_ln_b', 'rwkv_mu', 'rwkv_w0', 'rwkv_a0', 'rwkv_k_k', 'rwkv_k_a',
                 'rwkv_lnx_g', 'rwkv_lnx_b', 'ln1_g', 'ln1_b', 'ln2_g', 'ln2_b'):
        w[name] = a[name][l][None, :]
    w['rwkv_r_k'] = a['rwkv_r_k'][l].reshape(1, RWKV_WIDTH)
    pad = lambda m, lo: jnp.zeros((LANES, RWKV_WIDTH), F32).at[lo:lo + m.shape[0]].set(m).astype(BF16)
    w['w2p'] = pad(a['rwkv_w2'][l], 0)
    w['a2p'] = pad(a['rwkv_a2'][l], W_LORA)
    w['g2p'] = pad(a['rwkv_g2'][l], W_LORA + A_LORA)
    head_of = jnp.arange(RWKV_WIDTH) // RWKV_HEAD
    w['ones_bd'] = (head_of[:, None] == head_of[None, :]).astype(BF16)
    w_out = a['w_out'][l]
    w['wo_a'] = w_out[:MLA_WIDTH].astype(BF16)
    w['wo_c'] = w_out[MLA_WIDTH:MLA_WIDTH + CONV_CH].astype(BF16)
    w['wo_r'] = w_out[MLA_WIDTH + CONV_CH:].astype(BF16)
    w['ple_w'] = a['ple_w'][l].astype(BF16)
    w['ple_gate_w'] = a['ple_gate_w'][l].astype(BF16)
    return w


def _rope_tables(pos):
    half = MLA_ROPE // 2
    inv = ROPE_BASE ** (-jnp.arange(half, dtype=F32) / half)
    ang = pos.astype(F32)[:, None] * inv[None, :]
    cos = jnp.cos(ang)
    sin = jnp.sin(ang)
    rep = lambda t: jnp.tile(jnp.concatenate([t, t], axis=-1), (1, MLA_HEADS))
    return rep(cos), rep(sin)


def _run_group(x, p_emb, pos, a, layer_w, mixers_w, depth, alpha, past):
    B, T, _ = x.shape
    n = B * T
    x = x.reshape(n, D_MODEL)
    cos_t, sin_t = _rope_tables(pos)
    chains = B * RWKV_HEADS
    assert LANES % chains == 0 and RWKV_HEAD % (LANES // chains) == 0
    i_lo = LANES // chains
    news = []
    for l in range(depth):
        w = layer_w[l]
        q, kvb, ckv, kr, u, rcols = _proj(x, cos_t, sin_t, w, T)
        if past is None:
            o = _attn_prompt(q, kvb, B, T)
            conv_buf = jnp.zeros((B, CONV_W - 1, CONV_CH), F32)
            shift_buf = jnp.zeros((B, RWKV_COLS), F32)
            wkv0 = jnp.zeros((B, RWKV_HEADS, RWKV_HEAD, RWKV_HEAD), F32)
        else:
            rows = MLA_HEADS * T
            q4 = q.reshape(MLA_HEADS, B, T, QK_WIDTH).transpose(1, 0, 2, 3)
            ql = q4[..., :KV_LORA].reshape(B, rows, KV_LORA)
            qr = jnp.stack([q4[:, h, :, KV_LORA + h * MLA_ROPE:KV_LORA + (h + 1) * MLA_ROPE]
                            for h in range(MLA_HEADS)], axis=1).reshape(B, rows, MLA_ROPE)
            o = _attn_sample(ql, qr, ckv.reshape(B, T, KV_LORA), kr.reshape(B, T, MLA_ROPE),
                             past['cache_ckv'], past['cache_kr'], past['page_table'], l)
            o = o.reshape(B, MLA_HEADS, T, KV_LORA).transpose(1, 0, 2, 3).reshape(MLA_HEADS, n, KV_LORA).astype(BF16)
            conv_buf, shift_buf, wkv0 = past['state_conv'][l], past['state_shift'][l], past['state_wkv'][l]
        attn = _uvnorm(o, w['wuvp'], w['mla_out_g'])
        conv, conv_new = _conv(u.reshape(B, T, CONV_CH), conv_buf, w['conv_w'], w['conv_b'],
                               w['conv_ln_g'], w['conv_ln_b'])
        rc3 = rcols.reshape(B, T, RWKV_COLS)
        shifted = jnp.concatenate([shift_buf[:, None, :], rc3[:, :-1]], axis=1).reshape(n, RWKV_COLS)
        (r, dec, nkk, kka, kp, v, g, bonus), feature_major = _rwkv_prep(rcols, shifted, w, B, T)
        if feature_major:
            ops = [_scan_in(t, B, T, i_lo, True) for t in (dec, nkk, kka, kp, r)]
            v_rows = _scan_in(v, B, T, i_lo, False)
        else:
            ops = [_to_key_tiles(t, B, T, i_lo) for t in (dec, nkk, kka, kp, r)]
            v_rows = _to_val_rows(v, B, T, i_lo)
        y_rows, s_fin = _scan(ops, v_rows, _state_to_tiles(wkv0, i_lo))
        y = _scan_out(y_rows, B, T, i_lo) if feature_major else _from_val_rows(y_rows, B, T, i_lo)
        x1 = _outproj(x, attn, conv.reshape(n, CONV_CH), y, bonus, g, w, alpha, T, feature_major)
        j = l // 2
        p_l = p_emb[l].reshape(n, -1)
        if l % 2 == 0:
            x = _ffn(x1, p_l, mixers_w['ffn_g'][j], mixers_w['ffn_u'][j], mixers_w['ffn_d'][j], w, alpha)
        else:
            x = _moe(x1, p_l, mixers_w['router_t'][j], mixers_w['moe_g'], mixers_w['moe_u'],
                     mixers_w['moe_d'], j, w, alpha)
        news.append((ckv.reshape(B, T, KV_LORA), kr.reshape(B, T, MLA_ROPE), conv_new, rc3[:, -1],
                     _state_from_tiles(s_fin, B, i_lo)))
    outs = tuple(jnp.stack([st[i] for st in news]) for i in range(5))
    return (x.reshape(B, T, D_MODEL),) + outs


def kernel(x_prompt, x_sample, cache_ckv, cache_kr, state_conv, state_shift, state_wkv, page_table,
           p_prompt, p_sample, w_in, q_norm_g, w_uq, kv_norm_g, w_uk, w_uv, mla_out_g,
           conv_w, conv_b, conv_ln_g, conv_ln_b, rwkv_mu, rwkv_w0, rwkv_w2, rwkv_a0, rwkv_a2,
           rwkv_g2, rwkv_k_k, rwkv_k_a, rwkv_r_k, rwkv_lnx_g, rwkv_lnx_b, w_out, ln1_g, ln1_b,
           ffn_w_gate, ffn_w_up, ffn_w_down, moe_router, moe_w_gate, moe_w_up, moe_w_down,
           ln2_g, ln2_b, ple_w, ple_gate_w):
    a = dict(w_in=w_in, q_norm_g=q_norm_g, w_uq=w_uq, kv_norm_g=kv_norm_g, w_uk=w_uk, w_uv=w_uv,
             mla_out_g=mla_out_g, conv_w=conv_w, conv_b=conv_b, conv_ln_g=conv_ln_g, conv_ln_b=conv_ln_b,
             rwkv_mu=rwkv_mu, rwkv_w0=rwkv_w0, rwkv_w2=rwkv_w2, rwkv_a0=rwkv_a0, rwkv_a2=rwkv_a2,
             rwkv_g2=rwkv_g2, rwkv_k_k=rwkv_k_k, rwkv_k_a=rwkv_k_a, rwkv_r_k=rwkv_r_k,
             rwkv_lnx_g=rwkv_lnx_g, rwkv_lnx_b=rwkv_lnx_b, w_out=w_out, ln1_g=ln1_g, ln1_b=ln1_b,
             ln2_g=ln2_g, ln2_b=ln2_b, ple_w=ple_w, ple_gate_w=ple_gate_w)
    depth = w_in.shape[0]
    alpha = (2 * depth) ** 0.25
    layer_w = [_layer_weights(l, a) for l in range(depth)]
    mixers_w = dict(ffn_g=ffn_w_gate.astype(BF16), ffn_u=ffn_w_up.astype(BF16), ffn_d=ffn_w_down.astype(BF16),
                    router_t=jnp.pad(moe_router.transpose(0, 2, 1), ((0, 0), (0, ROUTER_ROWS - N_EXPERTS), (0, 0))),
                    moe_g=moe_w_gate.astype(BF16), moe_u=moe_w_up.astype(BF16), moe_d=moe_w_down.astype(BF16))
    n_pages = page_table.shape[1]
    past_len = n_pages * PAGE_SIZE
    pos_prompt = jnp.arange(x_prompt.shape[1], dtype=jnp.int32)
    pos_sample = past_len + jnp.arange(x_sample.shape[1], dtype=jnp.int32)
    past = dict(cache_ckv=cache_ckv, cache_kr=cache_kr.transpose(0, 1, 3, 2), page_table=page_table,
                state_conv=state_conv, state_shift=state_shift, state_wkv=state_wkv)
    yp = _run_group(x_prompt, p_prompt, pos_prompt, a, layer_w, mixers_w, depth, alpha, None)
    ys = _run_group(x_sample, p_sample, pos_sample, a, layer_w, mixers_w, depth, alpha, past)
    return (yp[0], ys[0]) + yp[1:] + ys[1:]
```

```python
import functools

import jax
import jax.numpy as jnp
from jax import lax
from jax.experimental import pallas as pl
from jax.experimental.pallas import tpu as pltpu

F32 = jnp.float32
BF16 = jnp.bfloat16

D_MODEL = 1024
MLA_HEADS = 8
MLA_NOPE = 64
MLA_ROPE = 32
MLA_V = 64
MLA_WIDTH = MLA_HEADS * MLA_V
Q_LORA = 384
KV_LORA = 256
ROPE_BASE = 10000.0
MLA_SCALE = (MLA_NOPE + MLA_ROPE) ** -0.5
CONV_CH = 256
CONV_W = 31
RWKV_WIDTH = 256
RWKV_HEAD = 64
RWKV_HEADS = 4
W_LORA = 32
A_LORA = 32
G_LORA = 64
RWKV_COLS = 3 * RWKV_WIDTH + W_LORA + A_LORA + G_LORA
LNX_EPS = 64e-5
N_EXPERTS = 8
TOP_K = 2
PAGE_SIZE = 128
LN_EPS = 1e-5
RMS_EPS = 1e-6

LANES = 128
SUBLANES = 8
VMEM_LIMIT_BYTES = 56 * 1024 * 1024
QK_WIDTH = 2 * KV_LORA

ROW_TILE = 512
ATTN_TQ = 128
ATTN_TK = 512
ATTN_ROW_CHUNK = 64
LOG2E = 1.4426950408889634
Q_SCALE = MLA_SCALE * LOG2E
PAGES_PER_STEP = 16
NEW_ROWS_PAD = 16
SCAN_STEPS = 64
MOE_ROWS = 512
MOE_FF_TILE = 1792
GATHER_ROWS = 256
GATHER_UNROLL = 8
ROUTER_ROWS = 16


def _tile(n, pref, mult=SUBLANES):
    if n <= pref:
        return n
    for t in range(pref, 0, -1):
        if n % t == 0 and t % mult == 0:
            return t
    return n


def _params(*sem):
    return pltpu.CompilerParams(dimension_semantics=sem, vmem_limit_bytes=VMEM_LIMIT_BYTES)


def _dot(a, b):
    return jnp.dot(a, b, preferred_element_type=F32)


def _dot_nt(a, b):
    return lax.dot_general(a, b, (((1,), (1,)), ((), ())), preferred_element_type=F32)


def _split3(x):
    hi = x.astype(BF16)
    r1 = x - hi.astype(F32)
    mid = r1.astype(BF16)
    lo = (r1 - mid.astype(F32)).astype(BF16)
    return hi, mid, lo


def _segsum(x, ones_bd):
    hi, mid, lo = _split3(x)
    return _dot(hi, ones_bd) + _dot(mid, ones_bd) + _dot(lo, ones_bd)


def _layer_norm_rows(z, g, b):
    mu = jnp.mean(z, axis=-1, keepdims=True)
    d = z - mu
    var = jnp.mean(d * d, axis=-1, keepdims=True)
    return d * lax.rsqrt(var + LN_EPS) * g + b


def _rms_norm_rows(z, g):
    return z * lax.rsqrt(jnp.mean(z * z, axis=-1, keepdims=True) + RMS_EPS) * g


def _sigmoid(z):
    return 1.0 / (1.0 + jnp.exp(-z))


def _full(shape):
    nd = len(shape)
    return pl.BlockSpec(shape, lambda *_: (0,) * nd)


def _proj_kernel(x_ref, cos_ref, sin_ref, wq_ref, wkv_ref, wkr_ref, wkrr_ref, wcv_ref, wrw_ref,
                 gq_ref, gkv_ref, wuqn_ref, wuqr_ref, wuqrr_ref, wukp_ref,
                 q_ref, kvb_ref, ckv_ref, kr_ref, u_ref, rc_ref):
    xb = x_ref[...].astype(BF16)
    cos = cos_ref[...]
    sin = sin_ref[...]
    cq = _rms_norm_rows(_dot(xb, wq_ref[...]), gq_ref[...]).astype(BF16)
    qn = _dot(cq, wuqn_ref[...]).astype(BF16)
    qr = (_dot(cq, wuqr_ref[...]) * cos + _dot(cq, wuqrr_ref[...]) * sin) * Q_SCALE
    lane = lax.broadcasted_iota(jnp.int32, qr.shape, 1)
    for h in range(MLA_HEADS):
        slab = qn[:, (h // 2) * LANES:(h // 2 + 1) * LANES]
        q_ref[h, :, 0:KV_LORA] = (_dot(slab, wukp_ref[h]) * Q_SCALE).astype(BF16)
        own = (lane >= h * MLA_ROPE) & (lane < (h + 1) * MLA_ROPE)
        q_ref[h, :, KV_LORA:QK_WIDTH] = jnp.where(own, qr, 0.0).astype(BF16)
    ckv = _rms_norm_rows(_dot(xb, wkv_ref[...]), gkv_ref[...])
    ckv_ref[...] = ckv
    kvb_ref[:, 0:KV_LORA] = ckv.astype(BF16)
    kr8 = _dot(xb, wkr_ref[...]) * cos + _dot(xb, wkrr_ref[...]) * sin
    kvb_ref[:, KV_LORA:QK_WIDTH] = kr8.astype(BF16)
    kr_ref[...] = kr8[:, 0:MLA_ROPE]
    cv = _dot(xb, wcv_ref[...])
    u_ref[...] = cv[:, 0:CONV_CH] * _sigmoid(cv[:, CONV_CH:2 * CONV_CH])
    rc_ref[...] = _dot(xb, wrw_ref[...])


def _proj(x, cos_t, sin_t, w, T):
    n = x.shape[0]
    tm = _tile(n, ROW_TILE)
    if T % tm == 0:
        nt = T // tm
        tab_map = lambda i: (i % nt, 0)
    else:
        assert tm % T == 0
        cos_t = jnp.tile(cos_t, (tm // T, 1))
        sin_t = jnp.tile(sin_t, (tm // T, 1))
        tab_map = lambda i: (0, 0)
    row = lambda c: pl.BlockSpec((tm, c), lambda i: (i, 0))
    wts = [w['wq'], w['wkv'], w['wkr8'], w['wkrr8'], w['wcv'], w['wrw'], w['gq'], w['gkv'],
           w['wuqn'], w['wuqr'], w['wuqrr'], w['wukp']]
    return pl.pallas_call(
        _proj_kernel,
        grid=(n // tm,),
        in_specs=[row(D_MODEL), pl.BlockSpec((tm, KV_LORA), tab_map), pl.BlockSpec((tm, KV_LORA), tab_map)]
                 + [_full(a.shape) for a in wts],
        out_specs=[pl.BlockSpec((MLA_HEADS, tm, QK_WIDTH), lambda i: (0, i, 0)),
                   row(QK_WIDTH), row(KV_LORA), row(MLA_ROPE), row(CONV_CH), row(RWKV_COLS)],
        out_shape=[jax.ShapeDtypeStruct((MLA_HEADS, n, QK_WIDTH), BF16),
                   jax.ShapeDtypeStruct((n, QK_WIDTH), BF16),
                   jax.ShapeDtypeStruct((n, KV_LORA), F32),
                   jax.ShapeDtypeStruct((n, MLA_ROPE), F32),
                   jax.ShapeDtypeStruct((n, CONV_CH), F32),
                   jax.ShapeDtypeStruct((n, RWKV_COLS), F32)],
        compiler_params=_params("parallel"),
        name="proj",
    )(x, cos_t, sin_t, *wts)


def _attn_prompt_kernel(q_ref, kv_ref, o_ref, s_ref, p_ref, m_ref, l_ref, a_ref, acc_ref, *, tq, tk, rc):
    qi = pl.program_id(1)
    rows = MLA_HEADS * tq
    q = q_ref[...].reshape(rows, QK_WIDTH)
    m_ref[...] = jnp.full(m_ref.shape, -jnp.inf, F32)
    l_ref[...] = jnp.zeros(l_ref.shape, F32)
    acc_ref[...] = jnp.zeros(acc_ref.shape, F32)

    half = rows // 2

    def block(k0, width, masked):
        kv = kv_ref[pl.ds(k0, width), :]
        s_ref[0:half, 0:width] = _dot_nt(q[0:half], kv)
        s_ref[half:rows, 0:width] = _dot_nt(q[half:rows], kv)

        def scores(c):
            s = s_ref[pl.ds(c * rc, rc), 0:width]
            if masked:
                q_pos = qi * tq + ((c * rc + lax.broadcasted_iota(jnp.int32, (rc, 1), 0)) & (tq - 1))
                k_pos = k0 + lax.broadcasted_iota(jnp.int32, (1, width), 1)
                s = jnp.where(k_pos <= q_pos, s, -jnp.inf)
            return [s[:, g * LANES:(g + 1) * LANES] for g in range(width // LANES)]

        for c in range(rows // rc):
            sl = pl.ds(c * rc, rc)
            mx = functools.reduce(jnp.maximum, scores(c))
            m_prev = m_ref[sl, :]
            m_new = jnp.maximum(m_prev, jnp.max(mx, axis=-1, keepdims=True))
            a_ref[sl, :] = jnp.exp2(m_prev - m_new)
            m_ref[sl, :] = m_new
        for c in range(rows // rc):
            sl = pl.ds(c * rc, rc)
            m_cur = m_ref[sl, :]
            ps = [jnp.exp2(g - m_cur) for g in scores(c)]
            alpha = a_ref[sl, :]
            l_ref[sl, :] = alpha * l_ref[sl, :] + functools.reduce(jnp.add, ps)
            p_ref[sl, 0:width] = jnp.concatenate(ps, axis=-1).astype(BF16)
            acc_ref[sl, :] = acc_ref[sl, :] * jnp.concatenate([alpha] * (KV_LORA // LANES), axis=-1)
        vals = kv[:, 0:KV_LORA]
        acc_ref[0:half, :] += _dot(p_ref[0:half, 0:width], vals)
        acc_ref[half:rows, :] += _dot(p_ref[half:rows, 0:width], vals)

    q0 = qi * tq
    n_full = q0 // tk

    def full_block(kb, carry):
        block(pl.multiple_of(kb * tk, tk), tk, False)
        return carry

    lax.fori_loop(0, n_full, full_block, 0)
    lead = (q0 - n_full * tk) // tq
    for r in range(tk // tq):
        @pl.when(lead == r)
        def _(r=r):
            block(pl.multiple_of(n_full * tk, tk), (r + 1) * tq, True)

    o = acc_ref[...] / jnp.sum(l_ref[...], axis=-1, keepdims=True)
    o_ref[...] = o.reshape(MLA_HEADS, tq, KV_LORA).astype(BF16)


def _attn_prompt(q, kvb, B, T):
    n = B * T
    tq = _tile(T, ATTN_TQ, 16)
    tk = _tile(T, ATTN_TK, 16)
    assert tq & (tq - 1) == 0 and tk % tq == 0
    nq = T // tq
    rows = MLA_HEADS * tq
    rc = _tile(rows, ATTN_ROW_CHUNK, 16)
    return pl.pallas_call(
        functools.partial(_attn_prompt_kernel, tq=tq, tk=tk, rc=rc),
        grid=(B, nq),
        in_specs=[pl.BlockSpec((MLA_HEADS, tq, QK_WIDTH), lambda b, i: (0, b * nq + i, 0)),
                  pl.BlockSpec((T, QK_WIDTH), lambda b, i: (b, 0))],
        out_specs=pl.BlockSpec((MLA_HEADS, tq, KV_LORA), lambda b, i: (0, b * nq + i, 0)),
        out_shape=jax.ShapeDtypeStruct((MLA_HEADS, n, KV_LORA), BF16),
        scratch_shapes=[pltpu.VMEM((rows, tk), F32), pltpu.VMEM((rows, tk), BF16),
                        pltpu.VMEM((rows, LANES), F32), pltpu.VMEM((rows, LANES), F32),
                        pltpu.VMEM((rows, LANES), F32), pltpu.VMEM((rows, KV_LORA), F32)],
        compiler_params=_params("parallel", "parallel"),
        name="attn_prompt",
    )(q, kvb)


def _attn_sample_kernel(pt_ref, ql_ref, qr_ref, *refs, n_pg, n_new):
    del pt_ref
    c_pages = refs[:n_pg]
    r_pages = refs[n_pg:2 * n_pg]
    cn_ref, rn_ref, o_ref, m_ref, l_ref, acc_ref = refs[2 * n_pg:]
    g = pl.program_id(1)

    @pl.when(g == 0)
    def _():
        m_ref[...] = jnp.full(m_ref.shape, -jnp.inf, F32)
        l_ref[...] = jnp.zeros(l_ref.shape, F32)
        acc_ref[...] = jnp.zeros(acc_ref.shape, F32)

    ql = ql_ref[0]
    qr = qr_ref[0]

    def update(s, vals):
        m_prev = m_ref[...]
        m_new = jnp.maximum(m_prev, jnp.max(s, axis=-1, keepdims=True))
        p = jnp.exp2(s - m_new)
        alpha = jnp.exp2(m_prev - m_new)
        l_ref[...] = alpha * l_ref[...] + jnp.sum(p, axis=-1, keepdims=True)
        acc_ref[...] = alpha * acc_ref[...] + _dot(p.astype(BF16), vals)
        m_ref[...] = m_new

    cs = [c_pages[k][0, 0].astype(BF16) for k in range(n_pg)]
    s = jnp.concatenate([_dot_nt(ql, cs[k]) + _dot(qr, r_pages[k][0, 0].astype(BF16))
                         for k in range(n_pg)], axis=-1)
    update(s, jnp.concatenate(cs, axis=0))

    @pl.when(g == pl.num_programs(1) - 1)
    def _():
        cn = cn_ref[0].astype(BF16)
        s_new = _dot_nt(ql, cn) + _dot_nt(qr, rn_ref[0].astype(BF16))
        t_q = lax.broadcasted_iota(jnp.int32, s_new.shape, 0) & (n_new - 1)
        t_k = lax.broadcasted_iota(jnp.int32, s_new.shape, 1)
        update(jnp.where(t_k <= t_q, s_new, -jnp.inf), cn)
        o_ref[0] = acc_ref[...] / l_ref[...]


def _attn_sample(ql, qr, ckv_new, kr_new, cache_ckv, cache_kr_t, page_table, layer):
    B, rows, _ = ql.shape
    n_new = ckv_new.shape[1]
    assert n_new & (n_new - 1) == 0
    n_pad = max(NEW_ROWS_PAD, n_new)
    ckv_new = jnp.pad(ckv_new, ((0, 0), (0, n_pad - n_new), (0, 0)))
    kr_new = jnp.pad(kr_new, ((0, 0), (0, n_pad - n_new), (0, 0)))
    n_pages = page_table.shape[1]
    n_pg = _tile(n_pages, PAGES_PER_STEP, 1)
    n_g = n_pages // n_pg

    def page_spec(shape, k):
        return pl.BlockSpec((1, 1) + shape, lambda b, g, pt: (layer, pt[b * n_pages + g * n_pg + k], 0, 0))

    grid_spec = pltpu.PrefetchScalarGridSpec(
        num_scalar_prefetch=1,
        grid=(B, n_g),
        in_specs=[pl.BlockSpec((1, rows, KV_LORA), lambda b, g, pt: (b, 0, 0)),
                  pl.BlockSpec((1, rows, MLA_ROPE), lambda b, g, pt: (b, 0, 0))]
                 + [page_spec((PAGE_SIZE, KV_LORA), k) for k in range(n_pg)]
                 + [page_spec((MLA_ROPE, PAGE_SIZE), k) for k in range(n_pg)]
                 + [pl.BlockSpec((1, n_pad, KV_LORA), lambda b, g, pt: (b, 0, 0)),
                    pl.BlockSpec((1, n_pad, MLA_ROPE), lambda b, g, pt: (b, 0, 0))],
        out_specs=pl.BlockSpec((1, rows, KV_LORA), lambda b, g, pt: (b, 0, 0)),
        scratch_shapes=[pltpu.VMEM((rows, 1), F32), pltpu.VMEM((rows, 1), F32),
                        pltpu.VMEM((rows, KV_LORA), F32)],
    )
    return pl.pallas_call(
        functools.partial(_attn_sample_kernel, n_pg=n_pg, n_new=n_new),
        grid_spec=grid_spec,
        out_shape=jax.ShapeDtypeStruct((B, rows, KV_LORA), F32),
        compiler_params=_params("parallel", "arbitrary"),
        name="attn_sample",
    )(page_table.reshape(-1), ql, qr, *([cache_ckv] * n_pg), *([cache_kr_t] * n_pg), ckv_new, kr_new)


def _uvnorm_kernel(o_ref, wuvp_ref, g_ref, out_ref):
    parts = []
    for p in range(MLA_HEADS // 2):
        pair = jnp.concatenate([o_ref[2 * p], o_ref[2 * p + 1]], axis=-1)
        parts.append(_dot(pair, wuvp_ref[p]))
    a = jnp.concatenate(parts, axis=-1)
    out_ref[...] = _rms_norm_rows(a, g_ref[...]).astype(BF16)


def _uvnorm(o, wuvp, g):
    n = o.shape[1]
    tm = _tile(n, ROW_TILE)
    return pl.pallas_call(
        _uvnorm_kernel,
        grid=(n // tm,),
        in_specs=[pl.BlockSpec((MLA_HEADS, tm, KV_LORA), lambda i: (0, i, 0)), _full(wuvp.shape), _full(g.shape)],
        out_specs=pl.BlockSpec((tm, MLA_WIDTH), lambda i: (i, 0)),
        out_shape=jax.ShapeDtypeStruct((n, MLA_WIDTH), BF16),
        compiler_params=_params("parallel"),
        name="uv_norm",
    )(o, wuvp, g)


CONV_PAD = 32


def _conv_kernel(u_ref, buf_ref, w_ref, b_ref, g_ref, bb_ref, y_ref, new_ref, up_ref, sh_ref, *, T, tt):
    lead = CONV_PAD - (CONV_W - 1)
    up_ref[lead:CONV_PAD, :] = buf_ref[0]
    up_ref[CONV_PAD:CONV_PAD + T, :] = u_ref[0]
    for t0 in range(0, T, tt):
        acc = jnp.zeros((tt, CONV_CH), F32) + b_ref[...]
        for b in range(min(SUBLANES, CONV_W)):
            taps = range(b, CONV_W, SUBLANES)
            rows = taps[-1] - b + tt
            sh_ref[0:rows, :] = up_ref[lead + t0 + b:lead + t0 + b + rows, :]
            for k in taps:
                acc = acc + sh_ref[k - b:k - b + tt, :] * w_ref[k:k + 1, :]
        z = _layer_norm_rows(acc, g_ref[...], bb_ref[...])
        y_ref[0, t0:t0 + tt, :] = z * _sigmoid(z)
    new_ref[0] = up_ref[lead + T:CONV_PAD + T, :]


def _conv(u, buf, w, b, g, bb):
    B, T, _ = u.shape
    tt = _tile(T, 256)
    return pl.pallas_call(
        functools.partial(_conv_kernel, T=T, tt=tt),
        grid=(B,),
        in_specs=[pl.BlockSpec((1, T, CONV_CH), lambda i: (i, 0, 0)),
                  pl.BlockSpec((1, CONV_W - 1, CONV_CH), lambda i: (i, 0, 0)),
                  _full(w.shape), _full(b.shape), _full(g.shape), _full(bb.shape)],
        out_specs=[pl.BlockSpec((1, T, CONV_CH), lambda i: (i, 0, 0)),
                   pl.BlockSpec((1, CONV_W - 1, CONV_CH), lambda i: (i, 0, 0))],
        out_shape=[jax.ShapeDtypeStruct((B, T, CONV_CH), F32),
                   jax.ShapeDtypeStruct((B, CONV_W - 1, CONV_CH), F32)],
        scratch_shapes=[pltpu.VMEM((CONV_PAD + T, CONV_CH), F32),
                        pltpu.VMEM((tt + CONV_PAD, CONV_CH), F32)],
        compiler_params=_params("parallel"),
        name="conv_module",
    )(u, buf, w, b, g, bb)


def _rwkv_prep_kernel(c_ref, s_ref, mu_ref, w0_ref, a0_ref, kk_ref, ka_ref, rk_ref,
                      w2_ref, a2_ref, g2_ref, ones_ref,
                      r_out, w_out, nkk_out, kka_out, k_out, v_out, g_out, bonus_out, *, feature_major):
    def put(ref, val):
        ref[...] = val.T if feature_major else val

    cols = c_ref[...]
    xm = cols + (s_ref[...] - cols) * mu_ref[...]
    W = RWKV_WIDTH
    r = xm[:, 0:W]
    k = xm[:, W:2 * W]
    v = xm[:, 2 * W:3 * W]
    tail = xm[:, 3 * W:3 * W + LANES]
    ones_bd = ones_ref[...]
    z = w0_ref[...] + _dot(jnp.tanh(tail).astype(BF16), w2_ref[...])
    softplus = jnp.maximum(-z, 0.0) + jnp.log(1.0 + jnp.exp(-jnp.abs(z)))
    decay = jnp.exp(-jnp.exp(-softplus - 0.5))
    a = _sigmoid(a0_ref[...] + _dot(tail.astype(BF16), a2_ref[...]))
    g = _dot(_sigmoid(tail).astype(BF16), g2_ref[...])
    kk = k * kk_ref[...]
    norm = jnp.sqrt(_segsum(kk * kk, ones_bd))
    kk = kk / jnp.maximum(norm, 1e-12)
    kp = k * (1.0 + (a - 1.0) * ka_ref[...])
    put(r_out, r)
    put(w_out, decay)
    put(nkk_out, -kk)
    put(kka_out, kk * a)
    put(k_out, kp)
    put(v_out, v)
    g_out[...] = g
    bonus_out[...] = _segsum(r * kp * rk_ref[...], ones_bd) * v


def _rwkv_prep(cols, shifted, w, B, T):
    n = cols.shape[0]
    tm = _tile(n, ROW_TILE)
    feature_major = T % tm == 0 and tm % LANES == 0
    row = lambda c: pl.BlockSpec((tm, c), lambda i: (i, 0))
    wts = [w['rwkv_mu'], w['rwkv_w0'], w['rwkv_a0'], w['rwkv_k_k'], w['rwkv_k_a'], w['rwkv_r_k'],
           w['w2p'], w['a2p'], w['g2p'], w['ones_bd']]
    if feature_major:
        nt = T // tm
        op_spec = pl.BlockSpec((RWKV_WIDTH, tm), lambda i: (i // nt, i % nt))
        op_shape = jax.ShapeDtypeStruct((B * RWKV_WIDTH, T), F32)
    else:
        op_spec = row(RWKV_WIDTH)
        op_shape = jax.ShapeDtypeStruct((n, RWKV_WIDTH), F32)
    outs = pl.pallas_call(
        functools.partial(_rwkv_prep_kernel, feature_major=feature_major),
        grid=(n // tm,),
        in_specs=[row(RWKV_COLS), row(RWKV_COLS)] + [_full(a.shape) for a in wts],
        out_specs=[op_spec] * 6 + [row(RWKV_WIDTH)] * 2,
        out_shape=[op_shape] * 6 + [jax.ShapeDtypeStruct((n, RWKV_WIDTH), F32)] * 2,
        compiler_params=_params("parallel"),
        name="rwkv_prep",
    )(cols, shifted, *wts)
    return outs, feature_major


def _sublane_allsum(x):
    x = x + pltpu.roll(x, 4, 0)
    x = x + pltpu.roll(x, 2, 0)
    return x + pltpu.roll(x, 1, 0)


def _scan_kernel(w_ref, nkk_ref, kka_ref, k_ref, r_ref, v_ref, s0_ref, y_ref, s_ref, *, n_steps, n_ih, unroll):
    @pl.when(pl.program_id(0) == 0)
    def _():
        s_ref[...] = s0_ref[...]

    nv = RWKV_HEAD // SUBLANES
    shape3 = (nv, SUBLANES, LANES)

    def key_tile(ref, t):
        return ref[t].reshape(shape3)

    def step(t, carry):
        def group(gi, c2):
            for u in range(unroll):
                ih = gi * unroll + u
                s = s_ref[ih].reshape(shape3)
                sa = _sublane_allsum(jnp.sum(s * key_tile(nkk_ref, t), axis=0))
                v_row = v_ref[ih, pl.ds(t, 1), :]
                s = s * key_tile(w_ref, t) + key_tile(kka_ref, t) * sa[None] + key_tile(k_ref, t) * v_row[None]
                s_ref[ih] = s.reshape(RWKV_HEAD, LANES)
                y = _sublane_allsum(jnp.sum(s * key_tile(r_ref, t), axis=0))
                y_ref[ih, pl.ds(t, 1), :] = y[0:1, :]
            return c2

        lax.fori_loop(0, n_ih // unroll, group, 0)
        return carry

    lax.fori_loop(0, n_steps, step, 0)


def _scan(ops, v_rows, s0):
    n_ih, T, _ = v_rows.shape
    tc = _tile(T, SCAN_STEPS, SUBLANES)
    unroll = _tile(n_ih, 16, 1)
    key_spec = pl.BlockSpec((tc, RWKV_HEAD, LANES), lambda i: (i, 0, 0))
    val_spec = pl.BlockSpec((n_ih, tc, LANES), lambda i: (0, i, 0))
    return pl.pallas_call(
        functools.partial(_scan_kernel, n_steps=tc, n_ih=n_ih, unroll=unroll),
        grid=(T // tc,),
        in_specs=[key_spec] * 5 + [val_spec, _full(s0.shape)],
        out_specs=[val_spec, _full(s0.shape)],
        out_shape=[jax.ShapeDtypeStruct((n_ih, T, LANES), F32), jax.ShapeDtypeStruct(s0.shape, F32)],
        compiler_params=_params("arbitrary"),
        name="rwkv_scan",
    )(*ops, v_rows, s0)


def _scan_in_kernel(src_ref, dst_ref, *, chains, i_lo, key_operand):
    def rows_of(idx):
        return src_ref[pl.ds(idx, chains, stride=RWKV_HEAD), :]

    if key_operand:
        for j in range(RWKV_HEAD):
            dst_ref[:, j, :] = jnp.concatenate([rows_of(j)] * i_lo, axis=0).T
    else:
        for ih in range(RWKV_HEAD // i_lo):
            dst_ref[ih] = jnp.concatenate([rows_of(ih * i_lo + q) for q in range(i_lo)], axis=0).T


def _scan_in(x, B, T, i_lo, key_operand):
    chains = B * RWKV_HEADS
    n_ih = RWKV_HEAD // i_lo
    tt = LANES
    if key_operand:
        out_spec = pl.BlockSpec((tt, RWKV_HEAD, LANES), lambda i: (i, 0, 0))
        out_shape = jax.ShapeDtypeStruct((T, RWKV_HEAD, LANES), F32)
    else:
        out_spec = pl.BlockSpec((n_ih, tt, LANES), lambda i: (0, i, 0))
        out_shape = jax.ShapeDtypeStruct((n_ih, T, LANES), F32)
    return pl.pallas_call(
        functools.partial(_scan_in_kernel, chains=chains, i_lo=i_lo, key_operand=key_operand),
        grid=(T // tt,),
        in_specs=[pl.BlockSpec((chains * RWKV_HEAD, tt), lambda i: (0, i))],
        out_specs=out_spec,
        out_shape=out_shape,
        compiler_params=_params("parallel"),
        name="rwkv_scan_in",
    )(x)


def _scan_out_kernel(y_ref, out_ref, *, chains, i_lo):
    for ih in range(RWKV_HEAD // i_lo):
        tile = y_ref[ih].T
        for q in range(i_lo):
            out_ref[pl.ds(ih * i_lo + q, chains, stride=RWKV_HEAD), :] = tile[q * chains:(q + 1) * chains, :]


def _scan_out(y_rows, B, T, i_lo):
    chains = B * RWKV_HEADS
    n_ih = RWKV_HEAD // i_lo
    tt = LANES
    return pl.pallas_call(
        functools.partial(_scan_out_kernel, chains=chains, i_lo=i_lo),
        grid=(T // tt,),
        in_specs=[pl.BlockSpec((n_ih, tt, LANES), lambda i: (0, i, 0))],
        out_specs=pl.BlockSpec((chains * RWKV_HEAD, tt), lambda i: (0, i)),
        out_shape=jax.ShapeDtypeStruct((chains * RWKV_HEAD, T), F32),
        compiler_params=_params("parallel"),
        name="rwkv_scan_out",
    )(y_rows)


def _to_key_tiles(x, B, T, i_lo):
    x = x.reshape(B, T, RWKV_HEADS, RWKV_HEAD).transpose(1, 3, 0, 2).reshape(T, RWKV_HEAD, B * RWKV_HEADS)
    return jnp.tile(x, (1, 1, i_lo))


def _to_val_rows(x, B, T, i_lo):
    n_ih = RWKV_HEAD // i_lo
    x = x.reshape(B, T, RWKV_HEADS, n_ih, i_lo).transpose(3, 1, 4, 0, 2)
    return x.reshape(n_ih, T, i_lo * B * RWKV_HEADS)


def _from_val_rows(y, B, T, i_lo):
    n_ih = RWKV_HEAD // i_lo
    y = y.reshape(n_ih, T, i_lo, B, RWKV_HEADS).transpose(3, 1, 4, 0, 2)
    return y.reshape(B * T, RWKV_WIDTH)


def _state_to_tiles(s, i_lo):
    B = s.shape[0]
    n_ih = RWKV_HEAD // i_lo
    s = s.reshape(B, RWKV_HEADS, n_ih, i_lo, RWKV_HEAD).transpose(2, 4, 3, 0, 1)
    return s.reshape(n_ih, RWKV_HEAD, i_lo * B * RWKV_HEADS)


def _state_from_tiles(s, B, i_lo):
    n_ih = RWKV_HEAD // i_lo
    s = s.reshape(n_ih, RWKV_HEAD, i_lo, B, RWKV_HEADS).transpose(3, 4, 0, 2, 1)
    return s.reshape(B, RWKV_HEADS, RWKV_HEAD, RWKV_HEAD)


def _outproj_kernel(x_ref, at_ref, cv_ref, y_ref, bonus_ref, g_ref, lg_ref, lb_ref, ones_ref,
                    wa_ref, wc_ref, wr_ref, n1g_ref, n1b_ref, out_ref, *, alpha, y_feature_major):
    ones_bd = ones_ref[...]
    y = y_ref[...].T if y_feature_major else y_ref[...]
    inv = 1.0 / RWKV_HEAD
    mu = _segsum(y, ones_bd) * inv
    d = y - mu
    var = _segsum(d * d, ones_bd) * inv
    yn = d * lax.rsqrt(var + LNX_EPS) * lg_ref[...] + lb_ref[...]
    rw = ((yn + bonus_ref[...]) * g_ref[...]).astype(BF16)
    h = _dot(at_ref[...], wa_ref[...]) + _dot(cv_ref[...].astype(BF16), wc_ref[...]) + _dot(rw, wr_ref[...])
    out_ref[...] = _layer_norm_rows(alpha * x_ref[...] + h, n1g_ref[...], n1b_ref[...])


def _outproj(x, attn, conv, y, bonus, g, w, alpha, T, y_feature_major):
    n = x.shape[0]
    tm = _tile(n, ROW_TILE)
    row = lambda c: pl.BlockSpec((tm, c), lambda i: (i, 0))
    wts = [w['rwkv_lnx_g'], w['rwkv_lnx_b'], w['ones_bd'], w['wo_a'], w['wo_c'], w['wo_r'], w['ln1_g'], w['ln1_b']]
    if y_feature_major:
        nt = T // tm
        y_spec = pl.BlockSpec((RWKV_WIDTH, tm), lambda i: (i // nt, i % nt))
    else:
        y_spec = row(RWKV_WIDTH)
    return pl.pallas_call(
        functools.partial(_outproj_kernel, alpha=alpha, y_feature_major=y_feature_major),
        grid=(n // tm,),
        in_specs=[row(D_MODEL), row(MLA_WIDTH), row(CONV_CH), y_spec, row(RWKV_WIDTH), row(RWKV_WIDTH)]
                 + [_full(a.shape) for a in wts],
        out_specs=row(D_MODEL),
        out_shape=jax.ShapeDtypeStruct((n, D_MODEL), F32),
        compiler_params=_params("parallel"),
        name="out_proj",
    )(x, attn, conv, y, bonus, g, *wts)


def _norm_ple(x1, f, p, n2g, n2b, wple, wgate, alpha):
    x2 = _layer_norm_rows(alpha * x1 + f, n2g, n2b)
    gate = _sigmoid(_dot(x2.astype(BF16), wgate))
    return x2 + gate * _dot(p.astype(BF16), wple)


def _ffn_kernel(x_ref, p_ref, wg_ref, wu_ref, wd_ref, n2g_ref, n2b_ref, wple_ref, wgate_ref,
                out_ref, xb_ref, acc_ref, *, alpha):
    f = pl.program_id(1)

    @pl.when(f == 0)
    def _():
        xb_ref[...] = x_ref[...].astype(BF16)
        acc_ref[...] = jnp.zeros(acc_ref.shape, F32)

    xb = xb_ref[...]
    gate = _dot(xb, wg_ref[...])
    hid = (gate * _sigmoid(gate) * _dot(xb, wu_ref[...])).astype(BF16)
    acc_ref[...] += _dot(hid, wd_ref[...])

    @pl.when(f == pl.num_programs(1) - 1)
    def _():
        out_ref[...] = _norm_ple(x_ref[...], acc_ref[...], p_ref[...], n2g_ref[...], n2b_ref[...],
                                 wple_ref[...], wgate_ref[...], alpha)


def _ffn(x, p, wg, wu, wd, w, alpha):
    n = x.shape[0]
    d_ff = wg.shape[1]
    tm = _tile(n, ROW_TILE)
    tf = _tile(d_ff, 1408, LANES)
    row = lambda c: pl.BlockSpec((tm, c), lambda i, f: (i, 0))
    wts = [w['ln2_g'], w['ln2_b'], w['ple_w'], w['ple_gate_w']]
    return pl.pallas_call(
        functools.partial(_ffn_kernel, alpha=alpha),
        grid=(n // tm, d_ff // tf),
        in_specs=[row(D_MODEL), row(p.shape[1]),
                  pl.BlockSpec((D_MODEL, tf), lambda i, f: (0, f)),
                  pl.BlockSpec((D_MODEL, tf), lambda i, f: (0, f)),
                  pl.BlockSpec((tf, D_MODEL), lambda i, f: (f, 0))] + [_full(a.shape) for a in wts],
        out_specs=row(D_MODEL),
        out_shape=jax.ShapeDtypeStruct((n, D_MODEL), F32),
        scratch_shapes=[pltpu.VMEM((tm, D_MODEL), BF16), pltpu.VMEM((tm, D_MODEL), F32)],
        compiler_params=_params("parallel", "arbitrary"),
        name="ffn_dense",
    )(x, p, wg, wu, wd, *wts)


def _router_kernel(x_ref, wr_ref, idx_ref, gate_ref):
    lt = _dot_nt(wr_ref[...].astype(BF16), x_ref[...].astype(BF16))
    e = lax.broadcasted_iota(jnp.int32, lt.shape, 0).astype(F32)
    lt = jnp.where(e < N_EXPERTS, lt, -jnp.inf)
    m1 = jnp.max(lt, axis=0, keepdims=True)
    i1 = jnp.min(jnp.where(lt == m1, e, float(N_EXPERTS)), axis=0, keepdims=True)
    lt2 = jnp.where(e == i1, -jnp.inf, lt)
    m2 = jnp.max(lt2, axis=0, keepdims=True)
    i2 = jnp.min(jnp.where(lt2 == m2, e, float(N_EXPERTS)), axis=0, keepdims=True)
    d = jnp.exp(m2 - m1)
    idx_ref[0:1, :] = i1.astype(jnp.int32)
    idx_ref[1:2, :] = i2.astype(jnp.int32)
    gate_ref[0:1, :] = 1.0 / (1.0 + d)
    gate_ref[1:2, :] = d / (1.0 + d)


def _router(x, wr_t):
    n = x.shape[0]
    tm = _tile(n, ROW_TILE, LANES)
    return pl.pallas_call(
        _router_kernel,
        grid=(n // tm,),
        in_specs=[pl.BlockSpec((tm, D_MODEL), lambda i: (i, 0)), _full(wr_t.shape)],
        out_specs=[pl.BlockSpec((TOP_K, tm), lambda i: (0, i)), pl.BlockSpec((TOP_K, tm), lambda i: (0, i))],
        out_shape=[jax.ShapeDtypeStruct((TOP_K, n), jnp.int32), jax.ShapeDtypeStruct((TOP_K, n), F32)],
        compiler_params=_params("parallel"),
        name="moe_router",
    )(x, wr_t)


def _row_copy(src_hbm, buf_ref, sem, r, src_row):
    return pltpu.make_async_copy(src_hbm.at[pl.ds(src_row, 1)], buf_ref.at[pl.ds(r, 1)], sem)


def _gather_start(idx_ref, base, src_hbm, buf_ref, sem, n_rows):
    def start(r, c):
        _row_copy(src_hbm, buf_ref, sem, r, idx_ref[base + r]).start()
        return c

    lax.fori_loop(0, n_rows, start, 0, unroll=GATHER_UNROLL)


def _gather_wait(src_hbm, buf_ref, sem, n_rows):
    def wait(r, c):
        _row_copy(src_hbm, buf_ref, sem, r, 0).wait()
        return c

    lax.fori_loop(0, n_rows, wait, 0, unroll=GATHER_UNROLL)


def _moe_gather_kernel(tok_ref, used_ref, x_hbm, out_ref, buf_ref, sem, *, rows):
    base = pl.program_id(0) * rows

    @pl.when(base < used_ref[0])
    def _():
        _gather_start(tok_ref, base, x_hbm, buf_ref, sem, rows)
        _gather_wait(x_hbm, buf_ref, sem, rows)
        out_ref[...] = buf_ref[...].astype(BF16)

    @pl.when(base >= used_ref[0])
    def _():
        out_ref[...] = jnp.zeros(out_ref.shape, BF16)


def _moe_gather(row_tok, used_rows, x):
    n_rows = row_tok.shape[0]
    rows = _tile(n_rows, GATHER_ROWS, 16)
    grid_spec = pltpu.PrefetchScalarGridSpec(
        num_scalar_prefetch=2,
        grid=(n_rows // rows,),
        in_specs=[pl.BlockSpec(memory_space=pl.ANY)],
        out_specs=pl.BlockSpec((rows, D_MODEL), lambda i, tok, used: (i, 0)),
        scratch_shapes=[pltpu.VMEM((rows, D_MODEL), F32), pltpu.SemaphoreType.DMA],
    )
    return pl.pallas_call(
        functools.partial(_moe_gather_kernel, rows=rows),
        grid_spec=grid_spec,
        out_shape=jax.ShapeDtypeStruct((n_rows, D_MODEL), BF16),
        compiler_params=_params("arbitrary"),
        name="moe_gather",
    )(row_tok, used_rows, x)


def _moe_ffn_kernel(be_ref, used_ref, x_ref, wg_ref, wu_ref, wd_ref, out_ref, acc_ref):
    del be_ref
    i = pl.program_id(0)
    f = pl.program_id(1)
    last_f = pl.num_programs(1) - 1

    @pl.when(i < used_ref[0])
    def _():
        @pl.when(f == 0)
        def _():
            acc_ref[...] = jnp.zeros(acc_ref.shape, F32)

        xb = x_ref[...]
        gate = _dot(xb, wg_ref[0, 0])
        hid = (gate * _sigmoid(gate) * _dot(xb, wu_ref[0, 0])).astype(BF16)
        acc_ref[...] += _dot(hid, wd_ref[0, 0])

        @pl.when(f == last_f)
        def _():
            out_ref[...] = acc_ref[...]

    @pl.when((i >= used_ref[0]) & (f == last_f))
    def _():
        out_ref[...] = jnp.zeros(out_ref.shape, F32)


def _moe_ffn(block_e, used_blocks, xs, wg, wu, wd, j, m_rows):
    n_rows = xs.shape[0]
    d_ff = wg.shape[3]
    tf = _tile(d_ff, MOE_FF_TILE, 2 * LANES)
    n_f = d_ff // tf

    def blk(i, used):
        return jnp.minimum(i, used[0] - 1)

    def ftile(i, f, used):
        return jnp.where(i < used[0], f, n_f - 1)

    grid_spec = pltpu.PrefetchScalarGridSpec(
        num_scalar_prefetch=2,
        grid=(n_rows // m_rows, n_f),
        in_specs=[pl.BlockSpec((m_rows, D_MODEL), lambda i, f, be, used: (blk(i, used), 0)),
                  pl.BlockSpec((1, 1, D_MODEL, tf), lambda i, f, be, used: (j, be[blk(i, used)], 0, ftile(i, f, used))),
                  pl.BlockSpec((1, 1, D_MODEL, tf), lambda i, f, be, used: (j, be[blk(i, used)], 0, ftile(i, f, used))),
                  pl.BlockSpec((1, 1, tf, D_MODEL), lambda i, f, be, used: (j, be[blk(i, used)], ftile(i, f, used), 0))],
        out_specs=pl.BlockSpec((m_rows, D_MODEL), lambda i, f, be, used: (i, 0)),
        scratch_shapes=[pltpu.VMEM((m_rows, D_MODEL), F32)],
    )
    return pl.pallas_call(
        _moe_ffn_kernel,
        grid_spec=grid_spec,
        out_shape=jax.ShapeDtypeStruct((n_rows, D_MODEL), F32),
        compiler_params=_params("parallel", "arbitrary"),
        name="moe_ffn",
    )(block_e, used_blocks, xs, wg, wu, wd)


def _moe_combine_kernel(pos0_ref, pos1_ref, rows_hbm, x_ref, p_ref, gate_ref, n2g_ref, n2b_ref, wple_ref, wgate_ref,
                        out_ref, b0_ref, b1_ref, sem0, sem1, *, tm, alpha):
    base = pl.program_id(0) * tm
    _gather_start(pos0_ref, base, rows_hbm, b0_ref, sem0, tm)
    _gather_start(pos1_ref, base, rows_hbm, b1_ref, sem1, tm)
    _gather_wait(rows_hbm, b0_ref, sem0, tm)
    _gather_wait(rows_hbm, b1_ref, sem1, tm)
    gts = gate_ref[...]
    f = b0_ref[...] * gts[:, 0:1] + b1_ref[...] * gts[:, 1:2]
    out_ref[...] = _norm_ple(x_ref[...], f, p_ref[...], n2g_ref[...], n2b_ref[...],
                             wple_ref[...], wgate_ref[...], alpha)


def _moe_combine(pos0, pos1, out_rows, x, p, gates, w, alpha):
    n = x.shape[0]
    tm = _tile(n, GATHER_ROWS)
    row = lambda c: pl.BlockSpec((tm, c), lambda i, a, b: (i, 0))
    wts = [w['ln2_g'], w['ln2_b'], w['ple_w'], w['ple_gate_w']]
    grid_spec = pltpu.PrefetchScalarGridSpec(
        num_scalar_prefetch=2,
        grid=(n // tm,),
        in_specs=[pl.BlockSpec(memory_space=pl.ANY), row(D_MODEL), row(p.shape[1]), row(TOP_K)]
                 + [_full(a.shape) for a in wts],
        out_specs=row(D_MODEL),
        scratch_shapes=[pltpu.VMEM((tm, D_MODEL), F32), pltpu.VMEM((tm, D_MODEL), F32),
                        pltpu.SemaphoreType.DMA, pltpu.SemaphoreType.DMA],
    )
    return pl.pallas_call(
        functools.partial(_moe_combine_kernel, tm=tm, alpha=alpha),
        grid_spec=grid_spec,
        out_shape=jax.ShapeDtypeStruct((n, D_MODEL), F32),
        compiler_params=_params("arbitrary"),
        name="moe_combine",
    )(pos0, pos1, out_rows, x, p, gates, *wts)


def _moe(x, p, wr_t, wg, wu, wd, j, w, alpha):
    n = x.shape[0]
    idx, gates = _router(x, wr_t)
    flat_e = idx.T.reshape(-1)
    onehot = (flat_e[:, None] == jnp.arange(N_EXPERTS, dtype=jnp.int32)[None, :]).astype(jnp.int32)
    csum = jnp.cumsum(onehot, axis=0)
    rank = jnp.take_along_axis(csum, flat_e[:, None], axis=1)[:, 0] - 1
    counts = csum[-1]
    m_rows = min(MOE_ROWS, max(SUBLANES * 2, n))
    padded = (counts + m_rows - 1) // m_rows * m_rows
    pad_end = jnp.cumsum(padded)
    dest = (pad_end - padded)[flat_e] + rank
    n_assign = n * TOP_K
    n_blocks = -(-n_assign // m_rows) + N_EXPERTS
    flat_tok = jnp.repeat(jnp.arange(n, dtype=jnp.int32), TOP_K)
    row_tok = jnp.zeros((n_blocks * m_rows,), jnp.int32).at[dest].set(flat_tok)
    block_e = jnp.minimum(jnp.searchsorted(pad_end, jnp.arange(n_blocks, dtype=jnp.int32) * m_rows, side='right'),
                          N_EXPERTS - 1).astype(jnp.int32)
    used_rows = pad_end[-1:].astype(jnp.int32)
    xs = _moe_gather(row_tok, used_rows, x)
    out_rows = _moe_ffn(block_e, used_rows // m_rows, xs, wg, wu, wd, j, m_rows)
    dest2 = dest.reshape(n, TOP_K)
    return _moe_combine(dest2[:, 0], dest2[:, 1], out_rows, x, p, gates.T, w, alpha)


def _rot_cols(wmat):
    half = MLA_ROPE // 2
    return jnp.concatenate([-wmat[..., half:], wmat[..., :half]], axis=-1)


def _layer_weights(l, a):
    o1 = Q_LORA
    o2 = o1 + KV_LORA
    o3 = o2 + MLA_ROPE
    o4 = o3 + 2 * CONV_CH
    w_in = a['w_in'][l]
    w = {}
    w['wq'] = w_in[:, :o1].astype(BF16)
    w['wkv'] = w_in[:, o1:o2].astype(BF16)
    wkr = w_in[:, o2:o3]
    w['wkr8'] = jnp.tile(wkr, (1, MLA_HEADS)).astype(BF16)
    w['wkrr8'] = jnp.tile(_rot_cols(wkr), (1, MLA_HEADS)).astype(BF16)
    w['wcv'] = w_in[:, o3:o4].astype(BF16)
    w['wrw'] = w_in[:, o4:].astype(BF16)
    w['gq'] = a['q_norm_g'][l][None, :]
    w['gkv'] = a['kv_norm_g'][l][None, :]
    wuq = a['w_uq'][l].reshape(Q_LORA, MLA_HEADS, MLA_NOPE + MLA_ROPE)
    w['wuqn'] = wuq[:, :, :MLA_NOPE].reshape(Q_LORA, MLA_HEADS * MLA_NOPE).astype(BF16)
    wuqr = wuq[:, :, MLA_NOPE:]
    w['wuqr'] = wuqr.reshape(Q_LORA, MLA_HEADS * MLA_ROPE).astype(BF16)
    w['wuqrr'] = _rot_cols(wuqr).reshape(Q_LORA, MLA_HEADS * MLA_ROPE).astype(BF16)
    wuk_t = a['w_uk'][l].transpose(1, 2, 0)
    zeros = jnp.zeros_like(wuk_t)
    even = jnp.concatenate([wuk_t, zeros], axis=1)
    odd = jnp.concatenate([zeros, wuk_t], axis=1)
    is_odd = (jnp.arange(MLA_HEADS) % 2 == 1)[:, None, None]
    w['wukp'] = jnp.where(is_odd, odd, even).astype(BF16)
    wuv = a['w_uv'][l].transpose(1, 0, 2)
    zv = jnp.zeros((MLA_HEADS // 2, KV_LORA, MLA_V), F32)
    top = jnp.concatenate([wuv[0::2], zv], axis=2)
    bot = jnp.concatenate([zv, wuv[1::2]], axis=2)
    w['wuvp'] = jnp.concatenate([top, bot], axis=1).astype(BF16)
    w['mla_out_g'] = a['mla_out_g'][l][None, :]
    w['conv_w'] = a['conv_w'][l]
    for name in ('conv_b', 'conv_ln_g', 'conv_ln_b', 'rwkv_mu', 'rwkv_w0', 'rwkv_a0', 'rwkv_k_k', 'rwkv_k_a',
                 'rwkv_lnx_g', 'rwkv_lnx_b', 'ln1_g', 'ln1_b', 'ln2_g', 'ln2_b'):
        w[name] = a[name][l][None, :]
    w['rwkv_r_k'] = a['rwkv_r_k'][l].reshape(1, RWKV_WIDTH)
    pad = lambda m, lo: jnp.zeros((LANES, RWKV_WIDTH), F32).at[lo:lo + m.shape[0]].set(m).astype(BF16)
    w['w2p'] = pad(a['rwkv_w2'][l], 0)
    w['a2p'] = pad(a['rwkv_a2'][l], W_LORA)
    w['g2p'] = pad(a['rwkv_g2'][l], W_LORA + A_LORA)
    head_of = jnp.arange(RWKV_WIDTH) // RWKV_HEAD
    w['ones_bd'] = (head_of[:, None] == head_of[None, :]).astype(BF16)
    w_out = a['w_out'][l]
    w['wo_a'] = w_out[:MLA_WIDTH].astype(BF16)
    w['wo_c'] = w_out[MLA_WIDTH:MLA_WIDTH + CONV_CH].astype(BF16)
    w['wo_r'] = w_out[MLA_WIDTH + CONV_CH:].astype(BF16)
    w['ple_w'] = a['ple_w'][l].astype(BF16)
    w['ple_gate_w'] = a['ple_gate_w'][l].astype(BF16)
    return w


def _rope_tables(pos):
    half = MLA_ROPE // 2
    inv = ROPE_BASE ** (-jnp.arange(half, dtype=F32) / half)
    ang = pos.astype(F32)[:, None] * inv[None, :]
    cos = jnp.cos(ang)
    sin = jnp.sin(ang)
    rep = lambda t: jnp.tile(jnp.concatenate([t, t], axis=-1), (1, MLA_HEADS))
    return rep(cos), rep(sin)


def _run_group(x, p_emb, pos, a, layer_w, mixers_w, depth, alpha, past):
    B, T, _ = x.shape
    n = B * T
    x = x.reshape(n, D_MODEL)
    cos_t, sin_t = _rope_tables(pos)
    chains = B * RWKV_HEADS
    assert LANES % chains == 0 and RWKV_HEAD % (LANES // chains) == 0
    i_lo = LANES // chains
    news = []
    for l in range(depth):
        w = layer_w[l]
        q, kvb, ckv, kr, u, rcols = _proj(x, cos_t, sin_t, w, T)
        if past is None:
            o = _attn_prompt(q, kvb, B, T)
            conv_buf = jnp.zeros((B, CONV_W - 1, CONV_CH), F32)
            shift_buf = jnp.zeros((B, RWKV_COLS), F32)
            wkv0 = jnp.zeros((B, RWKV_HEADS, RWKV_HEAD, RWKV_HEAD), F32)
        else:
            rows = MLA_HEADS * T
            q4 = q.reshape(MLA_HEADS, B, T, QK_WIDTH).transpose(1, 0, 2, 3)
            ql = q4[..., :KV_LORA].reshape(B, rows, KV_LORA)
            qr = jnp.stack([q4[:, h, :, KV_LORA + h * MLA_ROPE:KV_LORA + (h + 1) * MLA_ROPE]
                            for h in range(MLA_HEADS)], axis=1).reshape(B, rows, MLA_ROPE)
            o = _attn_sample(ql, qr, ckv.reshape(B, T, KV_LORA), kr.reshape(B, T, MLA_ROPE),
                             past['cache_ckv'], past['cache_kr'], past['page_table'], l)
            o = o.reshape(B, MLA_HEADS, T, KV_LORA).transpose(1, 0, 2, 3).reshape(MLA_HEADS, n, KV_LORA).astype(BF16)
            conv_buf, shift_buf, wkv0 = past['state_conv'][l], past['state_shift'][l], past['state_wkv'][l]
        attn = _uvnorm(o, w['wuvp'], w['mla_out_g'])
        conv, conv_new = _conv(u.reshape(B, T, CONV_CH), conv_buf, w['conv_w'], w['conv_b'],
                               w['conv_ln_g'], w['conv_ln_b'])
        rc3 = rcols.reshape(B, T, RWKV_COLS)
        shifted = jnp.concatenate([shift_buf[:, None, :], rc3[:, :-1]], axis=1).reshape(n, RWKV_COLS)
        (r, dec, nkk, kka, kp, v, g, bonus), feature_major = _rwkv_prep(rcols, shifted, w, B, T)
        if feature_major:
            ops = [_scan_in(t, B, T, i_lo, True) for t in (dec, nkk, kka, kp, r)]
            v_rows = _scan_in(v, B, T, i_lo, False)
        else:
            ops = [_to_key_tiles(t, B, T, i_lo) for t in (dec, nkk, kka, kp, r)]
            v_rows = _to_val_rows(v, B, T, i_lo)
        y_rows, s_fin = _scan(ops, v_rows, _state_to_tiles(wkv0, i_lo))
        y = _scan_out(y_rows, B, T, i_lo) if feature_major else _from_val_rows(y_rows, B, T, i_lo)
        x1 = _outproj(x, attn, conv.reshape(n, CONV_CH), y, bonus, g, w, alpha, T, feature_major)
        j = l // 2
        p_l = p_emb[l].reshape(n, -1)
        if l % 2 == 0:
            x = _ffn(x1, p_l, mixers_w['ffn_g'][j], mixers_w['ffn_u'][j], mixers_w['ffn_d'][j], w, alpha)
        else:
            x = _moe(x1, p_l, mixers_w['router_t'][j], mixers_w['moe_g'], mixers_w['moe_u'],
                     mixers_w['moe_d'], j, w, alpha)
        news.append((ckv.reshape(B, T, KV_LORA), kr.reshape(B, T, MLA_ROPE), conv_new, rc3[:, -1],
                     _state_from_tiles(s_fin, B, i_lo)))
    outs = tuple(jnp.stack([st[i] for st in news]) for i in range(5))
    return (x.reshape(B, T, D_MODEL),) + outs


def kernel(x_prompt, x_sample, cache_ckv, cache_kr, state_conv, state_shift, state_wkv, page_table,
           p_prompt, p_sample, w_in, q_norm_g, w_uq, kv_norm_g, w_uk, w_uv, mla_out_g,
           conv_w, conv_b, conv_ln_g, conv_ln_b, rwkv_mu, rwkv_w0, rwkv_w2, rwkv_a0, rwkv_a2,
           rwkv_g2, rwkv_k_k, rwkv_k_a, rwkv_r_k, rwkv_lnx_g, rwkv_lnx_b, w_out, ln1_g, ln1_b,
           ffn_w_gate, ffn_w_up, ffn_w_down, moe_router, moe_w_gate, moe_w_up, moe_w_down,
           ln2_g, ln2_b, ple_w, ple_gate_w):
    a = dict(w_in=w_in, q_norm_g=q_norm_g, w_uq=w_uq, kv_norm_g=kv_norm_g, w_uk=w_uk, w_uv=w_uv,
             mla_out_g=mla_out_g, conv_w=conv_w, conv_b=conv_b, conv_ln_g=conv_ln_g, conv_ln_b=conv_ln_b,
             rwkv_mu=rwkv_mu, rwkv_w0=rwkv_w0, rwkv_w2=rwkv_w2, rwkv_a0=rwkv_a0, rwkv_a2=rwkv_a2,
             rwkv_g2=rwkv_g2, rwkv_k_k=rwkv_k_k, rwkv_k_a=rwkv_k_a, rwkv_r_k=rwkv_r_k,
             rwkv_lnx_g=rwkv_lnx_g, rwkv_lnx_b=rwkv_lnx_b, w_out=w_out, ln1_g=ln1_g, ln1_b=ln1_b,
             ln2_g=ln2_g, ln2_b=ln2_b, ple_w=ple_w, ple_gate_w=ple_gate_w)
    depth = w_in.shape[0]
    alpha = (2 * depth) ** 0.25
    layer_w = [_layer_weights(l, a) for l in range(depth)]
    mixers_w = dict(ffn_g=ffn_w_gate.astype(BF16), ffn_u=ffn_w_up.astype(BF16), ffn_d=ffn_w_down.astype(BF16),
                    router_t=jnp.pad(moe_router.transpose(0, 2, 1), ((0, 0), (0, ROUTER_ROWS - N_EXPERTS), (0, 0))),
                    moe_g=moe_w_gate.astype(BF16), moe_u=moe_w_up.astype(BF16), moe_d=moe_w_down.astype(BF16))
    n_pages = page_table.shape[1]
    past_len = n_pages * PAGE_SIZE
    pos_prompt = jnp.arange(x_prompt.shape[1], dtype=jnp.int32)
    pos_sample = past_len + jnp.arange(x_sample.shape[1], dtype=jnp.int32)
    past = dict(cache_ckv=cache_ckv, cache_kr=cache_kr.transpose(0, 1, 3, 2), page_table=page_table,
                state_conv=state_conv, state_shift=state_shift, state_wkv=state_wkv)
    yp = _run_group(x_prompt, p_prompt, pos_prompt, a, layer_w, mixers_w, depth, alpha, None)
    ys = _run_group(x_sample, p_sample, pos_sample, a, layer_w, mixers_w, depth, alpha, past)
    return (yp[0], ys[0]) + yp[1:] + ys[1:]
```

```python
import functools

import jax
import jax.numpy as jnp
from jax import lax
from jax.experimental import pallas as pl
from jax.experimental.pallas import tpu as pltpu

F32 = jnp.float32
BF16 = jnp.bfloat16

D_MODEL = 1024
MLA_HEADS = 8
MLA_NOPE = 64
MLA_ROPE = 32
MLA_V = 64
MLA_WIDTH = MLA_HEADS * MLA_V
Q_LORA = 384
KV_LORA = 256
ROPE_BASE = 10000.0
MLA_SCALE = (MLA_NOPE + MLA_ROPE) ** -0.5
CONV_CH = 256
CONV_W = 31
RWKV_WIDTH = 256
RWKV_HEAD = 64
RWKV_HEADS = 4
W_LORA = 32
A_LORA = 32
G_LORA = 64
RWKV_COLS = 3 * RWKV_WIDTH + W_LORA + A_LORA + G_LORA
LNX_EPS = 64e-5
N_EXPERTS = 8
TOP_K = 2
PAGE_SIZE = 128
LN_EPS = 1e-5
RMS_EPS = 1e-6

LANES = 128
SUBLANES = 8
VMEM_LIMIT_BYTES = 56 * 1024 * 1024
QK_WIDTH = 2 * KV_LORA

ROW_TILE = 512
ATTN_TQ = 128
ATTN_TK = 512
ATTN_ROW_CHUNK = 64
LOG2E = 1.4426950408889634
Q_SCALE = MLA_SCALE * LOG2E
PAGES_PER_STEP = 16
NEW_ROWS_PAD = 16
SCAN_STEPS = 64
MOE_ROWS = 512
MOE_FF_TILE = 1792
GATHER_ROWS = 512
GATHER_UNROLL = 8
ROUTER_ROWS = 16


def _tile(n, pref, mult=SUBLANES):
    if n <= pref:
        return n
    for t in range(pref, 0, -1):
        if n % t == 0 and t % mult == 0:
            return t
    return n


def _params(*sem):
    return pltpu.CompilerParams(dimension_semantics=sem, vmem_limit_bytes=VMEM_LIMIT_BYTES)


def _dot(a, b):
    return jnp.dot(a, b, preferred_element_type=F32)


def _dot_nt(a, b):
    return lax.dot_general(a, b, (((1,), (1,)), ((), ())), preferred_element_type=F32)


def _split3(x):
    hi = x.astype(BF16)
    r1 = x - hi.astype(F32)
    mid = r1.astype(BF16)
    lo = (r1 - mid.astype(F32)).astype(BF16)
    return hi, mid, lo


def _segsum(x, ones_bd):
    hi, mid, lo = _split3(x)
    return _dot(hi, ones_bd) + _dot(mid, ones_bd) + _dot(lo, ones_bd)


def _layer_norm_rows(z, g, b):
    mu = jnp.mean(z, axis=-1, keepdims=True)
    d = z - mu
    var = jnp.mean(d * d, axis=-1, keepdims=True)
    return d * lax.rsqrt(var + LN_EPS) * g + b


def _rms_norm_rows(z, g):
    return z * lax.rsqrt(jnp.mean(z * z, axis=-1, keepdims=True) + RMS_EPS) * g


def _sigmoid(z):
    return 1.0 / (1.0 + jnp.exp(-z))


def _full(shape):
    nd = len(shape)
    return pl.BlockSpec(shape, lambda *_: (0,) * nd)


def _proj_kernel(x_ref, cos_ref, sin_ref, wq_ref, wkv_ref, wkr_ref, wkrr_ref, wcv_ref, wrw_ref,
                 gq_ref, gkv_ref, wuqn_ref, wuqr_ref, wuqrr_ref, wukp_ref,
                 q_ref, kvb_ref, ckv_ref, kr_ref, u_ref, rc_ref):
    xb = x_ref[...].astype(BF16)
    cos = cos_ref[...]
    sin = sin_ref[...]
    cq = _rms_norm_rows(_dot(xb, wq_ref[...]), gq_ref[...]).astype(BF16)
    qn = _dot(cq, wuqn_ref[...]).astype(BF16)
    qr = (_dot(cq, wuqr_ref[...]) * cos + _dot(cq, wuqrr_ref[...]) * sin) * Q_SCALE
    lane = lax.broadcasted_iota(jnp.int32, qr.shape, 1)
    for h in range(MLA_HEADS):
        slab = qn[:, (h // 2) * LANES:(h // 2 + 1) * LANES]
        q_ref[h, :, 0:KV_LORA] = (_dot(slab, wukp_ref[h]) * Q_SCALE).astype(BF16)
        own = (lane >= h * MLA_ROPE) & (lane < (h + 1) * MLA_ROPE)
        q_ref[h, :, KV_LORA:QK_WIDTH] = jnp.where(own, qr, 0.0).astype(BF16)
    ckv = _rms_norm_rows(_dot(xb, wkv_ref[...]), gkv_ref[...])
    ckv_ref[...] = ckv
    kvb_ref[:, 0:KV_LORA] = ckv.astype(BF16)
    kr8 = _dot(xb, wkr_ref[...]) * cos + _dot(xb, wkrr_ref[...]) * sin
    kvb_ref[:, KV_LORA:QK_WIDTH] = kr8.astype(BF16)
    kr_ref[...] = kr8[:, 0:MLA_ROPE]
    cv = _dot(xb, wcv_ref[...])
    u_ref[...] = cv[:, 0:CONV_CH] * _sigmoid(cv[:, CONV_CH:2 * CONV_CH])
    rc_ref[...] = _dot(xb, wrw_ref[...])


def _proj(x, cos_t, sin_t, w, T):
    n = x.shape[0]
    tm = _tile(n, ROW_TILE)
    if T % tm == 0:
        nt = T // tm
        tab_map = lambda i: (i % nt, 0)
    else:
        assert tm % T == 0
        cos_t = jnp.tile(cos_t, (tm // T, 1))
        sin_t = jnp.tile(sin_t, (tm // T, 1))
        tab_map = lambda i: (0, 0)
    row = lambda c: pl.BlockSpec((tm, c), lambda i: (i, 0))
    wts = [w['wq'], w['wkv'], w['wkr8'], w['wkrr8'], w['wcv'], w['wrw'], w['gq'], w['gkv'],
           w['wuqn'], w['wuqr'], w['wuqrr'], w['wukp']]
    return pl.pallas_call(
        _proj_kernel,
        grid=(n // tm,),
        in_specs=[row(D_MODEL), pl.BlockSpec((tm, KV_LORA), tab_map), pl.BlockSpec((tm, KV_LORA), tab_map)]
                 + [_full(a.shape) for a in wts],
        out_specs=[pl.BlockSpec((MLA_HEADS, tm, QK_WIDTH), lambda i: (0, i, 0)),
                   row(QK_WIDTH), row(KV_LORA), row(MLA_ROPE), row(CONV_CH), row(RWKV_COLS)],
        out_shape=[jax.ShapeDtypeStruct((MLA_HEADS, n, QK_WIDTH), BF16),
                   jax.ShapeDtypeStruct((n, QK_WIDTH), BF16),
                   jax.ShapeDtypeStruct((n, KV_LORA), F32),
                   jax.ShapeDtypeStruct((n, MLA_ROPE), F32),
                   jax.ShapeDtypeStruct((n, CONV_CH), F32),
                   jax.ShapeDtypeStruct((n, RWKV_COLS), F32)],
        compiler_params=_params("parallel"),
        name="proj",
    )(x, cos_t, sin_t, *wts)


def _attn_prompt_kernel(q_ref, kv_ref, o_ref, s_ref, p_ref, m_ref, l_ref, a_ref, acc_ref, *, tq, tk, rc):
    qi = pl.program_id(1)
    rows = MLA_HEADS * tq
    q = q_ref[...].reshape(rows, QK_WIDTH)
    m_ref[...] = jnp.full(m_ref.shape, -jnp.inf, F32)
    l_ref[...] = jnp.zeros(l_ref.shape, F32)
    acc_ref[...] = jnp.zeros(acc_ref.shape, F32)

    half = rows // 2

    def block(k0, width, masked):
        kv = kv_ref[pl.ds(k0, width), :]
        s_ref[0:half, 0:width] = _dot_nt(q[0:half], kv)
        s_ref[half:rows, 0:width] = _dot_nt(q[half:rows], kv)

        def scores(c):
            s = s_ref[pl.ds(c * rc, rc), 0:width]
            if masked:
                q_pos = qi * tq + ((c * rc + lax.broadcasted_iota(jnp.int32, (rc, 1), 0)) & (tq - 1))
                k_pos = k0 + lax.broadcasted_iota(jnp.int32, (1, width), 1)
                s = jnp.where(k_pos <= q_pos, s, -jnp.inf)
            return [s[:, g * LANES:(g + 1) * LANES] for g in range(width // LANES)]

        for c in range(rows // rc):
            sl = pl.ds(c * rc, rc)
            mx = functools.reduce(jnp.maximum, scores(c))
            m_prev = m_ref[sl, :]
            m_new = jnp.maximum(m_prev, jnp.max(mx, axis=-1, keepdims=True))
            a_ref[sl, :] = jnp.exp2(m_prev - m_new)
            m_ref[sl, :] = m_new
        for c in range(rows // rc):
            sl = pl.ds(c * rc, rc)
            m_cur = m_ref[sl, :]
            ps = [jnp.exp2(g - m_cur) for g in scores(c)]
            alpha = a_ref[sl, :]
            l_ref[sl, :] = alpha * l_ref[sl, :] + functools.reduce(jnp.add, ps)
            p_ref[sl, 0:width] = jnp.concatenate(ps, axis=-1).astype(BF16)
            acc_ref[sl, :] = acc_ref[sl, :] * jnp.concatenate([alpha] * (KV_LORA // LANES), axis=-1)
        vals = kv[:, 0:KV_LORA]
        acc_ref[0:half, :] += _dot(p_ref[0:half, 0:width], vals)
        acc_ref[half:rows, :] += _dot(p_ref[half:rows, 0:width], vals)

    q0 = qi * tq
    n_full = q0 // tk

    def full_block(kb, carry):
        block(pl.multiple_of(kb * tk, tk), tk, False)
        return carry

    lax.fori_loop(0, n_full, full_block, 0)
    lead = (q0 - n_full * tk) // tq
    for r in range(tk // tq):
        @pl.when(lead == r)
        def _(r=r):
            block(pl.multiple_of(n_full * tk, tk), (r + 1) * tq, True)

    o = acc_ref[...] / jnp.sum(l_ref[...], axis=-1, keepdims=True)
    o_ref[...] = o.reshape(MLA_HEADS, tq, KV_LORA).astype(BF16)


def _attn_prompt(q, kvb, B, T):
    n = B * T
    tq = _tile(T, ATTN_TQ, 16)
    tk = _tile(T, ATTN_TK, 16)
    assert tq & (tq - 1) == 0 and tk % tq == 0
    nq = T // tq
    rows = MLA_HEADS * tq
    rc = _tile(rows, ATTN_ROW_CHUNK, 16)
    return pl.pallas_call(
        functools.partial(_attn_prompt_kernel, tq=tq, tk=tk, rc=rc),
        grid=(B, nq),
        in_specs=[pl.BlockSpec((MLA_HEADS, tq, QK_WIDTH), lambda b, i: (0, b * nq + i, 0)),
                  pl.BlockSpec((T, QK_WIDTH), lambda b, i: (b, 0))],
        out_specs=pl.BlockSpec((MLA_HEADS, tq, KV_LORA), lambda b, i: (0, b * nq + i, 0)),
        out_shape=jax.ShapeDtypeStruct((MLA_HEADS, n, KV_LORA), BF16),
        scratch_shapes=[pltpu.VMEM((rows, tk), F32), pltpu.VMEM((rows, tk), BF16),
                        pltpu.VMEM((rows, LANES), F32), pltpu.VMEM((rows, LANES), F32),
                        pltpu.VMEM((rows, LANES), F32), pltpu.VMEM((rows, KV_LORA), F32)],
        compiler_params=_params("parallel", "parallel"),
        name="attn_prompt",
    )(q, kvb)


def _attn_sample_kernel(pt_ref, ql_ref, qr_ref, *refs, n_pg, n_new):
    del pt_ref
    c_pages = refs[:n_pg]
    r_pages = refs[n_pg:2 * n_pg]
    cn_ref, rn_ref, o_ref, m_ref, l_ref, acc_ref = refs[2 * n_pg:]
    g = pl.program_id(1)

    @pl.when(g == 0)
    def _():
        m_ref[...] = jnp.full(m_ref.shape, -jnp.inf, F32)
        l_ref[...] = jnp.zeros(l_ref.shape, F32)
        acc_ref[...] = jnp.zeros(acc_ref.shape, F32)

    ql = ql_ref[0]
    qr = qr_ref[0]

    def update(s, vals):
        m_prev = m_ref[...]
        m_new = jnp.maximum(m_prev, jnp.max(s, axis=-1, keepdims=True))
        p = jnp.exp2(s - m_new)
        alpha = jnp.exp2(m_prev - m_new)
        l_ref[...] = alpha * l_ref[...] + jnp.sum(p, axis=-1, keepdims=True)
        acc_ref[...] = alpha * acc_ref[...] + _dot(p.astype(BF16), vals)
        m_ref[...] = m_new

    cs = [c_pages[k][0, 0].astype(BF16) for k in range(n_pg)]
    s = jnp.concatenate([_dot_nt(ql, cs[k]) + _dot(qr, r_pages[k][0, 0].astype(BF16))
                         for k in range(n_pg)], axis=-1)
    update(s, jnp.concatenate(cs, axis=0))

    @pl.when(g == pl.num_programs(1) - 1)
    def _():
        cn = cn_ref[0].astype(BF16)
        s_new = _dot_nt(ql, cn) + _dot_nt(qr, rn_ref[0].astype(BF16))
        t_q = lax.broadcasted_iota(jnp.int32, s_new.shape, 0) & (n_new - 1)
        t_k = lax.broadcasted_iota(jnp.int32, s_new.shape, 1)
        update(jnp.where(t_k <= t_q, s_new, -jnp.inf), cn)
        o_ref[0] = acc_ref[...] / l_ref[...]


def _attn_sample(ql, qr, ckv_new, kr_new, cache_ckv, cache_kr_t, page_table, layer):
    B, rows, _ = ql.shape
    n_new = ckv_new.shape[1]
    assert n_new & (n_new - 1) == 0
    n_pad = max(NEW_ROWS_PAD, n_new)
    ckv_new = jnp.pad(ckv_new, ((0, 0), (0, n_pad - n_new), (0, 0)))
    kr_new = jnp.pad(kr_new, ((0, 0), (0, n_pad - n_new), (0, 0)))
    n_pages = page_table.shape[1]
    n_pg = _tile(n_pages, PAGES_PER_STEP, 1)
    n_g = n_pages // n_pg

    def page_spec(shape, k):
        return pl.BlockSpec((1, 1) + shape, lambda b, g, pt: (layer, pt[b * n_pages + g * n_pg + k], 0, 0))

    grid_spec = pltpu.PrefetchScalarGridSpec(
        num_scalar_prefetch=1,
        grid=(B, n_g),
        in_specs=[pl.BlockSpec((1, rows, KV_LORA), lambda b, g, pt: (b, 0, 0)),
                  pl.BlockSpec((1, rows, MLA_ROPE), lambda b, g, pt: (b, 0, 0))]
                 + [page_spec((PAGE_SIZE, KV_LORA), k) for k in range(n_pg)]
                 + [page_spec((MLA_ROPE, PAGE_SIZE), k) for k in range(n_pg)]
                 + [pl.BlockSpec((1, n_pad, KV_LORA), lambda b, g, pt: (b, 0, 0)),
                    pl.BlockSpec((1, n_pad, MLA_ROPE), lambda b, g, pt: (b, 0, 0))],
        out_specs=pl.BlockSpec((1, rows, KV_LORA), lambda b, g, pt: (b, 0, 0)),
        scratch_shapes=[pltpu.VMEM((rows, 1), F32), pltpu.VMEM((rows, 1), F32),
                        pltpu.VMEM((rows, KV_LORA), F32)],
    )
    return pl.pallas_call(
        functools.partial(_attn_sample_kernel, n_pg=n_pg, n_new=n_new),
        grid_spec=grid_spec,
        out_shape=jax.ShapeDtypeStruct((B, rows, KV_LORA), F32),
        compiler_params=_params("parallel", "arbitrary"),
        name="attn_sample",
    )(page_table.reshape(-1), ql, qr, *([cache_ckv] * n_pg), *([cache_kr_t] * n_pg), ckv_new, kr_new)


def _uvnorm_kernel(o_ref, wuvp_ref, g_ref, out_ref):
    parts = []
    for p in range(MLA_HEADS // 2):
        pair = jnp.concatenate([o_ref[2 * p], o_ref[2 * p + 1]], axis=-1)
        parts.append(_dot(pair, wuvp_ref[p]))
    a = jnp.concatenate(parts, axis=-1)
    out_ref[...] = _rms_norm_rows(a, g_ref[...]).astype(BF16)


def _uvnorm(o, wuvp, g):
    n = o.shape[1]
    tm = _tile(n, ROW_TILE)
    return pl.pallas_call(
        _uvnorm_kernel,
        grid=(n // tm,),
        in_specs=[pl.BlockSpec((MLA_HEADS, tm, KV_LORA), lambda i: (0, i, 0)), _full(wuvp.shape), _full(g.shape)],
        out_specs=pl.BlockSpec((tm, MLA_WIDTH), lambda i: (i, 0)),
        out_shape=jax.ShapeDtypeStruct((n, MLA_WIDTH), BF16),
        compiler_params=_params("parallel"),
        name="uv_norm",
    )(o, wuvp, g)


CONV_PAD = 32


def _conv_kernel(u_ref, buf_ref, w_ref, b_ref, g_ref, bb_ref, y_ref, new_ref, up_ref, sh_ref, *, T, tt):
    lead = CONV_PAD - (CONV_W - 1)
    up_ref[lead:CONV_PAD, :] = buf_ref[0]
    up_ref[CONV_PAD:CONV_PAD + T, :] = u_ref[0]
    for t0 in range(0, T, tt):
        acc = jnp.zeros((tt, CONV_CH), F32) + b_ref[...]
        for b in range(min(SUBLANES, CONV_W)):
            taps = range(b, CONV_W, SUBLANES)
            rows = taps[-1] - b + tt
            sh_ref[0:rows, :] = up_ref[lead + t0 + b:lead + t0 + b + rows, :]
            for k in taps:
                acc = acc + sh_ref[k - b:k - b + tt, :] * w_ref[k:k + 1, :]
        z = _layer_norm_rows(acc, g_ref[...], bb_ref[...])
        y_ref[0, t0:t0 + tt, :] = z * _sigmoid(z)
    new_ref[0] = up_ref[lead + T:CONV_PAD + T, :]


def _conv(u, buf, w, b, g, bb):
    B, T, _ = u.shape
    tt = _tile(T, 256)
    return pl.pallas_call(
        functools.partial(_conv_kernel, T=T, tt=tt),
        grid=(B,),
        in_specs=[pl.BlockSpec((1, T, CONV_CH), lambda i: (i, 0, 0)),
                  pl.BlockSpec((1, CONV_W - 1, CONV_CH), lambda i: (i, 0, 0)),
                  _full(w.shape), _full(b.shape), _full(g.shape), _full(bb.shape)],
        out_specs=[pl.BlockSpec((1, T, CONV_CH), lambda i: (i, 0, 0)),
                   pl.BlockSpec((1, CONV_W - 1, CONV_CH), lambda i: (i, 0, 0))],
        out_shape=[jax.ShapeDtypeStruct((B, T, CONV_CH), F32),
                   jax.ShapeDtypeStruct((B, CONV_W - 1, CONV_CH), F32)],
        scratch_shapes=[pltpu.VMEM((CONV_PAD + T, CONV_CH), F32),
                        pltpu.VMEM((tt + CONV_PAD, CONV_CH), F32)],
        compiler_params=_params("parallel"),
        name="conv_module",
    )(u, buf, w, b, g, bb)


def _rwkv_prep_kernel(c_ref, s_ref, mu_ref, w0_ref, a0_ref, kk_ref, ka_ref, rk_ref,
                      w2_ref, a2_ref, g2_ref, ones_ref,
                      r_out, w_out, nkk_out, kka_out, k_out, v_out, g_out, bonus_out, *, feature_major):
    def put(ref, val):
        ref[...] = val.T if feature_major else val

    cols = c_ref[...]
    xm = cols + (s_ref[...] - cols) * mu_ref[...]
    W = RWKV_WIDTH
    r = xm[:, 0:W]
    k = xm[:, W:2 * W]
    v = xm[:, 2 * W:3 * W]
    tail = xm[:, 3 * W:3 * W + LANES]
    ones_bd = ones_ref[...]
    z = w0_ref[...] + _dot(jnp.tanh(tail).astype(BF16), w2_ref[...])
    softplus = jnp.maximum(-z, 0.0) + jnp.log(1.0 + jnp.exp(-jnp.abs(z)))
    decay = jnp.exp(-jnp.exp(-softplus - 0.5))
    a = _sigmoid(a0_ref[...] + _dot(tail.astype(BF16), a2_ref[...]))
    g = _dot(_sigmoid(tail).astype(BF16), g2_ref[...])
    kk = k * kk_ref[...]
    norm = jnp.sqrt(_segsum(kk * kk, ones_bd))
    kk = kk / jnp.maximum(norm, 1e-12)
    kp = k * (1.0 + (a - 1.0) * ka_ref[...])
    put(r_out, r)
    put(w_out, decay)
    put(nkk_out, -kk)
    put(kka_out, kk * a)
    put(k_out, kp)
    put(v_out, v)
    g_out[...] = g
    bonus_out[...] = _segsum(r * kp * rk_ref[...], ones_bd) * v


def _rwkv_prep(cols, shifted, w, B, T):
    n = cols.shape[0]
    tm = _tile(n, ROW_TILE)
    feature_major = T % tm == 0 and tm % LANES == 0
    row = lambda c: pl.BlockSpec((tm, c), lambda i: (i, 0))
    wts = [w['rwkv_mu'], w['rwkv_w0'], w['rwkv_a0'], w['rwkv_k_k'], w['rwkv_k_a'], w['rwkv_r_k'],
           w['w2p'], w['a2p'], w['g2p'], w['ones_bd']]
    if feature_major:
        nt = T // tm
        op_spec = pl.BlockSpec((RWKV_WIDTH, tm), lambda i: (i // nt, i % nt))
        op_shape = jax.ShapeDtypeStruct((B * RWKV_WIDTH, T), F32)
    else:
        op_spec = row(RWKV_WIDTH)
        op_shape = jax.ShapeDtypeStruct((n, RWKV_WIDTH), F32)
    outs = pl.pallas_call(
        functools.partial(_rwkv_prep_kernel, feature_major=feature_major),
        grid=(n // tm,),
        in_specs=[row(RWKV_COLS), row(RWKV_COLS)] + [_full(a.shape) for a in wts],
        out_specs=[op_spec] * 6 + [row(RWKV_WIDTH)] * 2,
        out_shape=[op_shape] * 6 + [jax.ShapeDtypeStruct((n, RWKV_WIDTH), F32)] * 2,
        compiler_params=_params("parallel"),
        name="rwkv_prep",
    )(cols, shifted, *wts)
    return outs, feature_major


def _sublane_allsum(x):
    x = x + pltpu.roll(x, 4, 0)
    x = x + pltpu.roll(x, 2, 0)
    return x + pltpu.roll(x, 1, 0)


def _scan_kernel(w_ref, nkk_ref, kka_ref, k_ref, r_ref, v_ref, s0_ref, y_ref, s_ref, *, n_steps, n_ih, unroll):
    @pl.when(pl.program_id(0) == 0)
    def _():
        s_ref[...] = s0_ref[...]

    nv = RWKV_HEAD // SUBLANES
    shape3 = (nv, SUBLANES, LANES)

    def key_tile(ref, t):
        return ref[t].reshape(shape3)

    def step(t, carry):
        def group(gi, c2):
            for u in range(unroll):
                ih = gi * unroll + u
                s = s_ref[ih].reshape(shape3)
                sa = _sublane_allsum(jnp.sum(s * key_tile(nkk_ref, t), axis=0))
                v_row = v_ref[ih, pl.ds(t, 1), :]
                s = s * key_tile(w_ref, t) + key_tile(kka_ref, t) * sa[None] + key_tile(k_ref, t) * v_row[None]
                s_ref[ih] = s.reshape(RWKV_HEAD, LANES)
                y = _sublane_allsum(jnp.sum(s * key_tile(r_ref, t), axis=0))
                y_ref[ih, pl.ds(t, 1), :] = y[0:1, :]
            return c2

        lax.fori_loop(0, n_ih // unroll, group, 0)
        return carry

    lax.fori_loop(0, n_steps, step, 0)


def _scan(ops, v_rows, s0):
    n_ih, T, _ = v_rows.shape
    tc = _tile(T, SCAN_STEPS, SUBLANES)
    unroll = _tile(n_ih, 16, 1)
    key_spec = pl.BlockSpec((tc, RWKV_HEAD, LANES), lambda i: (i, 0, 0))
    val_spec = pl.BlockSpec((n_ih, tc, LANES), lambda i: (0, i, 0))
    return pl.pallas_call(
        functools.partial(_scan_kernel, n_steps=tc, n_ih=n_ih, unroll=unroll),
        grid=(T // tc,),
        in_specs=[key_spec] * 5 + [val_spec, _full(s0.shape)],
        out_specs=[val_spec, _full(s0.shape)],
        out_shape=[jax.ShapeDtypeStruct((n_ih, T, LANES), F32), jax.ShapeDtypeStruct(s0.shape, F32)],
        compiler_params=_params("arbitrary"),
        name="rwkv_scan",
    )(*ops, v_rows, s0)


def _scan_in_kernel(src_ref, dst_ref, *, chains, i_lo, key_operand):
    def rows_of(idx):
        return src_ref[pl.ds(idx, chains, stride=RWKV_HEAD), :]

    if key_operand:
        for j in range(RWKV_HEAD):
            dst_ref[:, j, :] = jnp.concatenate([rows_of(j)] * i_lo, axis=0).T
    else:
        for ih in range(RWKV_HEAD // i_lo):
            dst_ref[ih] = jnp.concatenate([rows_of(ih * i_lo + q) for q in range(i_lo)], axis=0).T


def _scan_in(x, B, T, i_lo, key_operand):
    chains = B * RWKV_HEADS
    n_ih = RWKV_HEAD // i_lo
    tt = LANES
    if key_operand:
        out_spec = pl.BlockSpec((tt, RWKV_HEAD, LANES), lambda i: (i, 0, 0))
        out_shape = jax.ShapeDtypeStruct((T, RWKV_HEAD, LANES), F32)
    else:
        out_spec = pl.BlockSpec((n_ih, tt, LANES), lambda i: (0, i, 0))
        out_shape = jax.ShapeDtypeStruct((n_ih, T, LANES), F32)
    return pl.pallas_call(
        functools.partial(_scan_in_kernel, chains=chains, i_lo=i_lo, key_operand=key_operand),
        grid=(T // tt,),
        in_specs=[pl.BlockSpec((chains * RWKV_HEAD, tt), lambda i: (0, i))],
        out_specs=out_spec,
        out_shape=out_shape,
        compiler_params=_params("parallel"),
        name="rwkv_scan_in",
    )(x)


def _scan_out_kernel(y_ref, out_ref, *, chains, i_lo):
    for ih in range(RWKV_HEAD // i_lo):
        tile = y_ref[ih].T
        for q in range(i_lo):
            out_ref[pl.ds(ih * i_lo + q, chains, stride=RWKV_HEAD), :] = tile[q * chains:(q + 1) * chains, :]


def _scan_out(y_rows, B, T, i_lo):
    chains = B * RWKV_HEADS
    n_ih = RWKV_HEAD // i_lo
    tt = LANES
    return pl.pallas_call(
        functools.partial(_scan_out_kernel, chains=chains, i_lo=i_lo),
        grid=(T // tt,),
        in_specs=[pl.BlockSpec((n_ih, tt, LANES), lambda i: (0, i, 0))],
        out_specs=pl.BlockSpec((chains * RWKV_HEAD, tt), lambda i: (0, i)),
        out_shape=jax.ShapeDtypeStruct((chains * RWKV_HEAD, T), F32),
        compiler_params=_params("parallel"),
        name="rwkv_scan_out",
    )(y_rows)


def _to_key_tiles(x, B, T, i_lo):
    x = x.reshape(B, T, RWKV_HEADS, RWKV_HEAD).transpose(1, 3, 0, 2).reshape(T, RWKV_HEAD, B * RWKV_HEADS)
    return jnp.tile(x, (1, 1, i_lo))


def _to_val_rows(x, B, T, i_lo):
    n_ih = RWKV_HEAD // i_lo
    x = x.reshape(B, T, RWKV_HEADS, n_ih, i_lo).transpose(3, 1, 4, 0, 2)
    return x.reshape(n_ih, T, i_lo * B * RWKV_HEADS)


def _from_val_rows(y, B, T, i_lo):
    n_ih = RWKV_HEAD // i_lo
    y = y.reshape(n_ih, T, i_lo, B, RWKV_HEADS).transpose(3, 1, 4, 0, 2)
    return y.reshape(B * T, RWKV_WIDTH)


def _state_to_tiles(s, i_lo):
    B = s.shape[0]
    n_ih = RWKV_HEAD // i_lo
    s = s.reshape(B, RWKV_HEADS, n_ih, i_lo, RWKV_HEAD).transpose(2, 4, 3, 0, 1)
    return s.reshape(n_ih, RWKV_HEAD, i_lo * B * RWKV_HEADS)


def _state_from_tiles(s, B, i_lo):
    n_ih = RWKV_HEAD // i_lo
    s = s.reshape(n_ih, RWKV_HEAD, i_lo, B, RWKV_HEADS).transpose(3, 4, 0, 2, 1)
    return s.reshape(B, RWKV_HEADS, RWKV_HEAD, RWKV_HEAD)


def _outproj_kernel(x_ref, at_ref, cv_ref, y_ref, bonus_ref, g_ref, lg_ref, lb_ref, ones_ref,
                    wa_ref, wc_ref, wr_ref, n1g_ref, n1b_ref, out_ref, *, alpha, y_feature_major):
    ones_bd = ones_ref[...]
    y = y_ref[...].T if y_feature_major else y_ref[...]
    inv = 1.0 / RWKV_HEAD
    mu = _segsum(y, ones_bd) * inv
    d = y - mu
    var = _segsum(d * d, ones_bd) * inv
    yn = d * lax.rsqrt(var + LNX_EPS) * lg_ref[...] + lb_ref[...]
    rw = ((yn + bonus_ref[...]) * g_ref[...]).astype(BF16)
    h = _dot(at_ref[...], wa_ref[...]) + _dot(cv_ref[...].astype(BF16), wc_ref[...]) + _dot(rw, wr_ref[...])
    out_ref[...] = _layer_norm_rows(alpha * x_ref[...] + h, n1g_ref[...], n1b_ref[...])


def _outproj(x, attn, conv, y, bonus, g, w, alpha, T, y_feature_major):
    n = x.shape[0]
    tm = _tile(n, ROW_TILE)
    row = lambda c: pl.BlockSpec((tm, c), lambda i: (i, 0))
    wts = [w['rwkv_lnx_g'], w['rwkv_lnx_b'], w['ones_bd'], w['wo_a'], w['wo_c'], w['wo_r'], w['ln1_g'], w['ln1_b']]
    if y_feature_major:
        nt = T // tm
        y_spec = pl.BlockSpec((RWKV_WIDTH, tm), lambda i: (i // nt, i % nt))
    else:
        y_spec = row(RWKV_WIDTH)
    return pl.pallas_call(
        functools.partial(_outproj_kernel, alpha=alpha, y_feature_major=y_feature_major),
        grid=(n // tm,),
        in_specs=[row(D_MODEL), row(MLA_WIDTH), row(CONV_CH), y_spec, row(RWKV_WIDTH), row(RWKV_WIDTH)]
                 + [_full(a.shape) for a in wts],
        out_specs=row(D_MODEL),
        out_shape=jax.ShapeDtypeStruct((n, D_MODEL), F32),
        compiler_params=_params("parallel"),
        name="out_proj",
    )(x, attn, conv, y, bonus, g, *wts)


def _norm_ple(x1, f, p, n2g, n2b, wple, wgate, alpha):
    x2 = _layer_norm_rows(alpha * x1 + f, n2g, n2b)
    gate = _sigmoid(_dot(x2.astype(BF16), wgate))
    return x2 + gate * _dot(p.astype(BF16), wple)


def _ffn_kernel(x_ref, p_ref, wg_ref, wu_ref, wd_ref, n2g_ref, n2b_ref, wple_ref, wgate_ref,
                out_ref, xb_ref, acc_ref, *, alpha):
    f = pl.program_id(1)

    @pl.when(f == 0)
    def _():
        xb_ref[...] = x_ref[...].astype(BF16)
        acc_ref[...] = jnp.zeros(acc_ref.shape, F32)

    xb = xb_ref[...]
    gate = _dot(xb, wg_ref[...])
    hid = (gate * _sigmoid(gate) * _dot(xb, wu_ref[...])).astype(BF16)
    acc_ref[...] += _dot(hid, wd_ref[...])

    @pl.when(f == pl.num_programs(1) - 1)
    def _():
        out_ref[...] = _norm_ple(x_ref[...], acc_ref[...], p_ref[...], n2g_ref[...], n2b_ref[...],
                                 wple_ref[...], wgate_ref[...], alpha)


def _ffn(x, p, wg, wu, wd, w, alpha):
    n = x.shape[0]
    d_ff = wg.shape[1]
    tm = _tile(n, ROW_TILE)
    tf = _tile(d_ff, 1408, LANES)
    row = lambda c: pl.BlockSpec((tm, c), lambda i, f: (i, 0))
    wts = [w['ln2_g'], w['ln2_b'], w['ple_w'], w['ple_gate_w']]
    return pl.pallas_call(
        functools.partial(_ffn_kernel, alpha=alpha),
        grid=(n // tm, d_ff // tf),
        in_specs=[row(D_MODEL), row(p.shape[1]),
                  pl.BlockSpec((D_MODEL, tf), lambda i, f: (0, f)),
                  pl.BlockSpec((D_MODEL, tf), lambda i, f: (0, f)),
                  pl.BlockSpec((tf, D_MODEL), lambda i, f: (f, 0))] + [_full(a.shape) for a in wts],
        out_specs=row(D_MODEL),
        out_shape=jax.ShapeDtypeStruct((n, D_MODEL), F32),
        scratch_shapes=[pltpu.VMEM((tm, D_MODEL), BF16), pltpu.VMEM((tm, D_MODEL), F32)],
        compiler_params=_params("parallel", "arbitrary"),
        name="ffn_dense",
    )(x, p, wg, wu, wd, *wts)


def _router_kernel(x_ref, wr_ref, idx_ref, gate_ref):
    lt = _dot_nt(wr_ref[...].astype(BF16), x_ref[...].astype(BF16))
    e = lax.broadcasted_iota(jnp.int32, lt.shape, 0).astype(F32)
    lt = jnp.where(e < N_EXPERTS, lt, -jnp.inf)
    m1 = jnp.max(lt, axis=0, keepdims=True)
    i1 = jnp.min(jnp.where(lt == m1, e, float(N_EXPERTS)), axis=0, keepdims=True)
    lt2 = jnp.where(e == i1, -jnp.inf, lt)
    m2 = jnp.max(lt2, axis=0, keepdims=True)
    i2 = jnp.min(jnp.where(lt2 == m2, e, float(N_EXPERTS)), axis=0, keepdims=True)
    d = jnp.exp(m2 - m1)
    idx_ref[0:1, :] = i1.astype(jnp.int32)
    idx_ref[1:2, :] = i2.astype(jnp.int32)
    gate_ref[0:1, :] = 1.0 / (1.0 + d)
    gate_ref[1:2, :] = d / (1.0 + d)


def _router(x, wr_t):
    n = x.shape[0]
    tm = _tile(n, ROW_TILE, LANES)
    return pl.pallas_call(
        _router_kernel,
        grid=(n // tm,),
        in_specs=[pl.BlockSpec((tm, D_MODEL), lambda i: (i, 0)), _full(wr_t.shape)],
        out_specs=[pl.BlockSpec((TOP_K, tm), lambda i: (0, i)), pl.BlockSpec((TOP_K, tm), lambda i: (0, i))],
        out_shape=[jax.ShapeDtypeStruct((TOP_K, n), jnp.int32), jax.ShapeDtypeStruct((TOP_K, n), F32)],
        compiler_params=_params("parallel"),
        name="moe_router",
    )(x, wr_t)


def _row_copy(src_hbm, buf_ref, sem, r, src_row):
    return pltpu.make_async_copy(src_hbm.at[pl.ds(src_row, 1)], buf_ref.at[pl.ds(r, 1)], sem)


def _gather_start(idx_ref, base, src_hbm, buf_ref, sem, n_rows):
    def start(r, c):
        _row_copy(src_hbm, buf_ref, sem, r, idx_ref[base + r]).start()
        return c

    lax.fori_loop(0, n_rows, start, 0, unroll=GATHER_UNROLL)


def _gather_wait(src_hbm, buf_ref, sem, n_rows):
    def wait(r, c):
        _row_copy(src_hbm, buf_ref, sem, r, 0).wait()
        return c

    lax.fori_loop(0, n_rows, wait, 0, unroll=GATHER_UNROLL)


def _moe_gather_kernel(tok_ref, used_ref, x_hbm, out_ref, buf_ref, sem, *, rows):
    base = pl.program_id(0) * rows

    @pl.when(base < used_ref[0])
    def _():
        _gather_start(tok_ref, base, x_hbm, buf_ref, sem, rows)
        _gather_wait(x_hbm, buf_ref, sem, rows)
        out_ref[...] = buf_ref[...].astype(BF16)

    @pl.when(base >= used_ref[0])
    def _():
        out_ref[...] = jnp.zeros(out_ref.shape, BF16)


def _moe_gather(row_tok, used_rows, x):
    n_rows = row_tok.shape[0]
    rows = _tile(n_rows, GATHER_ROWS, 16)
    grid_spec = pltpu.PrefetchScalarGridSpec(
        num_scalar_prefetch=2,
        grid=(n_rows // rows,),
        in_specs=[pl.BlockSpec(memory_space=pl.ANY)],
        out_specs=pl.BlockSpec((rows, D_MODEL), lambda i, tok, used: (i, 0)),
        scratch_shapes=[pltpu.VMEM((rows, D_MODEL), F32), pltpu.SemaphoreType.DMA],
    )
    return pl.pallas_call(
        functools.partial(_moe_gather_kernel, rows=rows),
        grid_spec=grid_spec,
        out_shape=jax.ShapeDtypeStruct((n_rows, D_MODEL), BF16),
        compiler_params=_params("arbitrary"),
        name="moe_gather",
    )(row_tok, used_rows, x)


def _moe_ffn_kernel(be_ref, used_ref, x_ref, wg_ref, wu_ref, wd_ref, out_ref, acc_ref):
    del be_ref
    i = pl.program_id(0)
    f = pl.program_id(1)
    last_f = pl.num_programs(1) - 1

    @pl.when(i < used_ref[0])
    def _():
        @pl.when(f == 0)
        def _():
            acc_ref[...] = jnp.zeros(acc_ref.shape, F32)

        xb = x_ref[...]
        gate = _dot(xb, wg_ref[0, 0])
        hid = (gate * _sigmoid(gate) * _dot(xb, wu_ref[0, 0])).astype(BF16)
        acc_ref[...] += _dot(hid, wd_ref[0, 0])

        @pl.when(f == last_f)
        def _():
            out_ref[...] = acc_ref[...]

    @pl.when((i >= used_ref[0]) & (f == last_f))
    def _():
        out_ref[...] = jnp.zeros(out_ref.shape, F32)


def _moe_ffn(block_e, used_blocks, xs, wg, wu, wd, j, m_rows):
    n_rows = xs.shape[0]
    d_ff = wg.shape[3]
    tf = _tile(d_ff, MOE_FF_TILE, 2 * LANES)
    n_f = d_ff // tf

    def blk(i, used):
        return jnp.minimum(i, used[0] - 1)

    def ftile(i, f, used):
        return jnp.where(i < used[0], f, n_f - 1)

    grid_spec = pltpu.PrefetchScalarGridSpec(
        num_scalar_prefetch=2,
        grid=(n_rows // m_rows, n_f),
        in_specs=[pl.BlockSpec((m_rows, D_MODEL), lambda i, f, be, used: (blk(i, used), 0)),
                  pl.BlockSpec((1, 1, D_MODEL, tf), lambda i, f, be, used: (j, be[blk(i, used)], 0, ftile(i, f, used))),
                  pl.BlockSpec((1, 1, D_MODEL, tf), lambda i, f, be, used: (j, be[blk(i, used)], 0, ftile(i, f, used))),
                  pl.BlockSpec((1, 1, tf, D_MODEL), lambda i, f, be, used: (j, be[blk(i, used)], ftile(i, f, used), 0))],
        out_specs=pl.BlockSpec((m_rows, D_MODEL), lambda i, f, be, used: (i, 0)),
        scratch_shapes=[pltpu.VMEM((m_rows, D_MODEL), F32)],
    )
    return pl.pallas_call(
        _moe_ffn_kernel,
        grid_spec=grid_spec,
        out_shape=jax.ShapeDtypeStruct((n_rows, D_MODEL), F32),
        compiler_params=_params("parallel", "arbitrary"),
        name="moe_ffn",
    )(block_e, used_blocks, xs, wg, wu, wd)


def _moe_combine_kernel(pos0_ref, pos1_ref, rows_hbm, x_ref, p_ref, gate_ref, n2g_ref, n2b_ref, wple_ref, wgate_ref,
                        out_ref, b0_ref, b1_ref, sem0, sem1, *, tm, alpha):
    base = pl.program_id(0) * tm
    _gather_start(pos0_ref, base, rows_hbm, b0_ref, sem0, tm)
    _gather_start(pos1_ref, base, rows_hbm, b1_ref, sem1, tm)
    _gather_wait(rows_hbm, b0_ref, sem0, tm)
    _gather_wait(rows_hbm, b1_ref, sem1, tm)
    gts = gate_ref[...]
    f = b0_ref[...] * gts[:, 0:1] + b1_ref[...] * gts[:, 1:2]
    out_ref[...] = _norm_ple(x_ref[...], f, p_ref[...], n2g_ref[...], n2b_ref[...],
                             wple_ref[...], wgate_ref[...], alpha)


def _moe_combine(pos0, pos1, out_rows, x, p, gates, w, alpha):
    n = x.shape[0]
    tm = _tile(n, GATHER_ROWS)
    row = lambda c: pl.BlockSpec((tm, c), lambda i, a, b: (i, 0))
    wts = [w['ln2_g'], w['ln2_b'], w['ple_w'], w['ple_gate_w']]
    grid_spec = pltpu.PrefetchScalarGridSpec(
        num_scalar_prefetch=2,
        grid=(n // tm,),
        in_specs=[pl.BlockSpec(memory_space=pl.ANY), row(D_MODEL), row(p.shape[1]), row(TOP_K)]
                 + [_full(a.shape) for a in wts],
        out_specs=row(D_MODEL),
        scratch_shapes=[pltpu.VMEM((tm, D_MODEL), F32), pltpu.VMEM((tm, D_MODEL), F32),
                        pltpu.SemaphoreType.DMA, pltpu.SemaphoreType.DMA],
    )
    return pl.pallas_call(
        functools.partial(_moe_combine_kernel, tm=tm, alpha=alpha),
        grid_spec=grid_spec,
        out_shape=jax.ShapeDtypeStruct((n, D_MODEL), F32),
        compiler_params=_params("arbitrary"),
        name="moe_combine",
    )(pos0, pos1, out_rows, x, p, gates, *wts)


def _moe(x, p, wr_t, wg, wu, wd, j, w, alpha):
    n = x.shape[0]
    idx, gates = _router(x, wr_t)
    flat_e = idx.T.reshape(-1)
    onehot = (flat_e[:, None] == jnp.arange(N_EXPERTS, dtype=jnp.int32)[None, :]).astype(jnp.int32)
    csum = jnp.cumsum(onehot, axis=0)
    rank = jnp.take_along_axis(csum, flat_e[:, None], axis=1)[:, 0] - 1
    counts = csum[-1]
    m_rows = min(MOE_ROWS, max(SUBLANES * 2, n))
    padded = (counts + m_rows - 1) // m_rows * m_rows
    pad_end = jnp.cumsum(padded)
    dest = (pad_end - padded)[flat_e] + rank
    n_assign = n * TOP_K
    n_blocks = -(-n_assign // m_rows) + N_EXPERTS
    flat_tok = jnp.repeat(jnp.arange(n, dtype=jnp.int32), TOP_K)
    row_tok = jnp.zeros((n_blocks * m_rows,), jnp.int32).at[dest].set(flat_tok)
    block_e = jnp.minimum(jnp.searchsorted(pad_end, jnp.arange(n_blocks, dtype=jnp.int32) * m_rows, side='right'),
                          N_EXPERTS - 1).astype(jnp.int32)
    used_rows = pad_end[-1:].astype(jnp.int32)
    xs = _moe_gather(row_tok, used_rows, x)
    out_rows = _moe_ffn(block_e, used_rows // m_rows, xs, wg, wu, wd, j, m_rows)
    dest2 = dest.reshape(n, TOP_K)
    return _moe_combine(dest2[:, 0], dest2[:, 1], out_rows, x, p, gates.T, w, alpha)


def _rot_cols(wmat):
    half = MLA_ROPE // 2
    return jnp.concatenate([-wmat[..., half:], wmat[..., :half]], axis=-1)


def _layer_weights(l, a):
    o1 = Q_LORA
    o2 = o1 + KV_LORA
    o3 = o2 + MLA_ROPE
    o4 = o3 + 2 * CONV_CH
    w_in = a['w_in'][l]
    w = {}
    w['wq'] = w_in[:, :o1].astype(BF16)
    w['wkv'] = w_in[:, o1:o2].astype(BF16)
    wkr = w_in[:, o2:o3]
    w['wkr8'] = jnp.tile(wkr, (1, MLA_HEADS)).astype(BF16)
    w['wkrr8'] = jnp.tile(_rot_cols(wkr), (1, MLA_HEADS)).astype(BF16)
    w['wcv'] = w_in[:, o3:o4].astype(BF16)
    w['wrw'] = w_in[:, o4:].astype(BF16)
    w['gq'] = a['q_norm_g'][l][None, :]
    w['gkv'] = a['kv_norm_g'][l][None, :]
    wuq = a['w_uq'][l].reshape(Q_LORA, MLA_HEADS, MLA_NOPE + MLA_ROPE)
    w['wuqn'] = wuq[:, :, :MLA_NOPE].reshape(Q_LORA, MLA_HEADS * MLA_NOPE).astype(BF16)
    wuqr = wuq[:, :, MLA_NOPE:]
    w['wuqr'] = wuqr.reshape(Q_LORA, MLA_HEADS * MLA_ROPE).astype(BF16)
    w['wuqrr'] = _rot_cols(wuqr).reshape(Q_LORA, MLA_HEADS * MLA_ROPE).astype(BF16)
    wuk_t = a['w_uk'][l].transpose(1, 2, 0)
    zeros = jnp.zeros_like(wuk_t)
    even = jnp.concatenate([wuk_t, zeros], axis=1)
    odd = jnp.concatenate([zeros, wuk_t], axis=1)
    is_odd = (jnp.arange(MLA_HEADS) % 2 == 1)[:, None, None]
    w['wukp'] = jnp.where(is_odd, odd, even).astype(BF16)
    wuv = a['w_uv'][l].transpose(1, 0, 2)
    zv = jnp.zeros((MLA_HEADS // 2, KV_LORA, MLA_V), F32)
    top = jnp.concatenate([wuv[0::2], zv], axis=2)
    bot = jnp.concatenate([zv, wuv[1::2]], axis=2)
    w['wuvp'] = jnp.concatenate([top, bot], axis=1).astype(BF16)
    w['mla_out_g'] = a['mla_out_g'][l][None, :]
    w['conv_w'] = a['conv_w'][l]
    for name in ('conv_b', 'conv_ln_g', 'conv_ln_b', 'rwkv_mu', 'rwkv_w0', 'rwkv_a0', 'rwkv_k_k', 'rwkv_k_a',
                 'rwkv_lnx_g', 'rwkv_lnx_b', 'ln1_g', 'ln1_b', 'ln2_g', 'ln2_b'):
        w[name] = a[name][l][None, :]
    w['rwkv_r_k'] = a['rwkv_r_k'][l].reshape(1, RWKV_WIDTH)
    pad = lambda m, lo: jnp.zeros((LANES, RWKV_WIDTH), F32).at[lo:lo + m.shape[0]].set(m).astype(BF16)
    w['w2p'] = pad(a['rwkv_w2'][l], 0)
    w['a2p'] = pad(a['rwkv_a2'][l], W_LORA)
    w['g2p'] = pad(a['rwkv_g2'][l], W_LORA + A_LORA)
    head_of = jnp.arange(RWKV_WIDTH) // RWKV_HEAD
    w['ones_bd'] = (head_of[:, None] == head_of[None, :]).astype(BF16)
    w_out = a['w_out'][l]
    w['wo_a'] = w_out[:MLA_WIDTH].astype(BF16)
    w['wo_c'] = w_out[MLA_WIDTH:MLA_WIDTH + CONV_CH].astype(BF16)
    w['wo_r'] = w_out[MLA_WIDTH + CONV_CH:].astype(BF16)
    w['ple_w'] = a['ple_w'][l].astype(BF16)
    w['ple_gate_w'] = a['ple_gate_w'][l].astype(BF16)
    return w


def _rope_tables(pos):
    half = MLA_ROPE // 2
    inv = ROPE_BASE ** (-jnp.arange(half, dtype=F32) / half)
    ang = pos.astype(F32)[:, None] * inv[None, :]
    cos = jnp.cos(ang)
    sin = jnp.sin(ang)
    rep = lambda t: jnp.tile(jnp.concatenate([t, t], axis=-1), (1, MLA_HEADS))
    return rep(cos), rep(sin)


def _run_group(x, p_emb, pos, a, layer_w, mixers_w, depth, alpha, past):
    B, T, _ = x.shape
    n = B * T
    x = x.reshape(n, D_MODEL)
    cos_t, sin_t = _rope_tables(pos)
    chains = B * RWKV_HEADS
    assert LANES % chains == 0 and RWKV_HEAD % (LANES // chains) == 0
    i_lo = LANES // chains
    news = []
    for l in range(depth):
        w = layer_w[l]
        q, kvb, ckv, kr, u, rcols = _proj(x, cos_t, sin_t, w, T)
        if past is None:
            o = _attn_prompt(q, kvb, B, T)
            conv_buf = jnp.zeros((B, CONV_W - 1, CONV_CH), F32)
            shift_buf = jnp.zeros((B, RWKV_COLS), F32)
            wkv0 = jnp.zeros((B, RWKV_HEADS, RWKV_HEAD, RWKV_HEAD), F32)
        else:
            rows = MLA_HEADS * T
            q4 = q.reshape(MLA_HEADS, B, T, QK_WIDTH).transpose(1, 0, 2, 3)
            ql = q4[..., :KV_LORA].reshape(B, rows, KV_LORA)
            qr = jnp.stack([q4[:, h, :, KV_LORA + h * MLA_ROPE:KV_LORA + (h + 1) * MLA_ROPE]
                            for h in range(MLA_HEADS)], axis=1).reshape(B, rows, MLA_ROPE)
            o = _attn_sample(ql, qr, ckv.reshape(B, T, KV_LORA), kr.reshape(B, T, MLA_ROPE),
                             past['cache_ckv'], past['cache_kr'], past['page_table'], l)
            o = o.reshape(B, MLA_HEADS, T, KV_LORA).transpose(1, 0, 2, 3).reshape(MLA_HEADS, n, KV_LORA).astype(BF16)
            conv_buf, shift_buf, wkv0 = past['state_conv'][l], past['state_shift'][l], past['state_wkv'][l]
        attn = _uvnorm(o, w['wuvp'], w['mla_out_g'])
        conv, conv_new = _conv(u.reshape(B, T, CONV_CH), conv_buf, w['conv_w'], w['conv_b'],
                               w['conv_ln_g'], w['conv_ln_b'])
        rc3 = rcols.reshape(B, T, RWKV_COLS)
        shifted = jnp.concatenate([shift_buf[:, None, :], rc3[:, :-1]], axis=1).reshape(n, RWKV_COLS)
        (r, dec, nkk, kka, kp, v, g, bonus), feature_major = _rwkv_prep(rcols, shifted, w, B, T)
        if feature_major:
            ops = [_scan_in(t, B, T, i_lo, True) for t in (dec, nkk, kka, kp, r)]
            v_rows = _scan_in(v, B, T, i_lo, False)
        else:
            ops = [_to_key_tiles(t, B, T, i_lo) for t in (dec, nkk, kka, kp, r)]
            v_rows = _to_val_rows(v, B, T, i_lo)
        y_rows, s_fin = _scan(ops, v_rows, _state_to_tiles(wkv0, i_lo))
        y = _scan_out(y_rows, B, T, i_lo) if feature_major else _from_val_rows(y_rows, B, T, i_lo)
        x1 = _outproj(x, attn, conv.reshape(n, CONV_CH), y, bonus, g, w, alpha, T, feature_major)
        j = l // 2
        p_l = p_emb[l].reshape(n, -1)
        if l % 2 == 0:
            x = _ffn(x1, p_l, mixers_w['ffn_g'][j], mixers_w['ffn_u'][j], mixers_w['ffn_d'][j], w, alpha)
        else:
            x = _moe(x1, p_l, mixers_w['router_t'][j], mixers_w['moe_g'], mixers_w['moe_u'],
                     mixers_w['moe_d'], j, w, alpha)
        news.append((ckv.reshape(B, T, KV_LORA), kr.reshape(B, T, MLA_ROPE), conv_new, rc3[:, -1],
                     _state_from_tiles(s_fin, B, i_lo)))
    outs = tuple(jnp.stack([st[i] for st in news]) for i in range(5))
    return (x.reshape(B, T, D_MODEL),) + outs


def kernel(x_prompt, x_sample, cache_ckv, cache_kr, state_conv, state_shift, state_wkv, page_table,
           p_prompt, p_sample, w_in, q_norm_g, w_uq, kv_norm_g, w_uk, w_uv, mla_out_g,
           conv_w, conv_b, conv_ln_g, conv_ln_b, rwkv_mu, rwkv_w0, rwkv_w2, rwkv_a0, rwkv_a2,
           rwkv_g2, rwkv_k_k, rwkv_k_a, rwkv_r_k, rwkv_lnx_g, rwkv_lnx_b, w_out, ln1_g, ln1_b,
           ffn_w_gate, ffn_w_up, ffn_w_down, moe_router, moe_w_gate, moe_w_up, moe_w_down,
           ln2_g, ln2_b, ple_w, ple_gate_w):
    a = dict(w_in=w_in, q_norm_g=q_norm_g, w_uq=w_uq, kv_norm_g=kv_norm_g, w_uk=w_uk, w_uv=w_uv,
             mla_out_g=mla_out_g, conv_w=conv_w, conv_b=conv_b, conv_ln_g=conv_ln_g, conv_ln_b=conv_ln_b,
             rwkv_mu=rwkv_mu, rwkv_w0=rwkv_w0, rwkv_w2=rwkv_w2, rwkv_a0=rwkv_a0, rwkv_a2=rwkv_a2,
             rwkv_g2=rwkv_g2, rwkv_k_k=rwkv_k_k, rwkv_k_a=rwkv_k_a, rwkv_r_k=rwkv_r_k,
             rwkv_lnx_g=rwkv_lnx_g, rwkv_lnx_b=rwkv_lnx_b, w_out=w_out, ln1_g=ln1_g, ln1_b=ln1_b,
             ln2_g=ln2_g, ln2_b=ln2_b, ple_w=ple_w, ple_gate_w=ple_gate_w)
    depth = w_in.shape[0]
    alpha = (2 * depth) ** 0.25
    layer_w = [_layer_weights(l, a) for l in range(depth)]
    mixers_w = dict(ffn_g=ffn_w_gate.astype(BF16), ffn_u=ffn_w_up.astype(BF16), ffn_d=ffn_w_down.astype(BF16),
                    router_t=jnp.pad(moe_router.transpose(0, 2, 1), ((0, 0), (0, ROUTER_ROWS - N_EXPERTS), (0, 0))),
                    moe_g=moe_w_gate.astype(BF16), moe_u=moe_w_up.astype(BF16), moe_d=moe_w_down.astype(BF16))
    n_pages = page_table.shape[1]
    past_len = n_pages * PAGE_SIZE
    pos_prompt = jnp.arange(x_prompt.shape[1], dtype=jnp.int32)
    pos_sample = past_len + jnp.arange(x_sample.shape[1], dtype=jnp.int32)
    past = dict(cache_ckv=cache_ckv, cache_kr=cache_kr.transpose(0, 1, 3, 2), page_table=page_table,
                state_conv=state_conv, state_shift=state_shift, state_wkv=state_wkv)
    yp = _run_group(x_prompt, p_prompt, pos_prompt, a, layer_w, mixers_w, depth, alpha, None)
    ys = _run_group(x_sample, p_sample, pos_sample, a, layer_w, mixers_w, depth, alpha, past)
    return (yp[0], ys[0]) + yp[1:] + ys[1:]
```

```python
import functools

import jax
import jax.numpy as jnp
from jax import lax
from jax.experimental import pallas as pl
from jax.experimental.pallas import tpu as pltpu

F32 = jnp.float32
BF16 = jnp.bfloat16

D_MODEL = 1024
MLA_HEADS = 8
MLA_NOPE = 64
MLA_ROPE = 32
MLA_V = 64
MLA_WIDTH = MLA_HEADS * MLA_V
Q_LORA = 384
KV_LORA = 256
ROPE_BASE = 10000.0
MLA_SCALE = (MLA_NOPE + MLA_ROPE) ** -0.5
CONV_CH = 256
CONV_W = 31
RWKV_WIDTH = 256
RWKV_HEAD = 64
RWKV_HEADS = 4
W_LORA = 32
A_LORA = 32
G_LORA = 64
RWKV_COLS = 3 * RWKV_WIDTH + W_LORA + A_LORA + G_LORA
LNX_EPS = 64e-5
N_EXPERTS = 8
TOP_K = 2
PAGE_SIZE = 128
LN_EPS = 1e-5
RMS_EPS = 1e-6

LANES = 128
SUBLANES = 8
VMEM_LIMIT_BYTES = 56 * 1024 * 1024
QK_WIDTH = 2 * KV_LORA

ROW_TILE = 512
ATTN_TQ = 128
ATTN_TK = 512
ATTN_ROW_CHUNK = 64
LOG2E = 1.4426950408889634
Q_SCALE = MLA_SCALE * LOG2E
PAGES_PER_STEP = 32
NEW_ROWS_PAD = 16
SCAN_STEPS = 64
MOE_ROWS = 512
MOE_FF_TILE = 1792
GATHER_ROWS = 512
GATHER_UNROLL = 8
ROUTER_ROWS = 16


def _tile(n, pref, mult=SUBLANES):
    if n <= pref:
        return n
    for t in range(pref, 0, -1):
        if n % t == 0 and t % mult == 0:
            return t
    return n


def _params(*sem):
    return pltpu.CompilerParams(dimension_semantics=sem, vmem_limit_bytes=VMEM_LIMIT_BYTES)


def _dot(a, b):
    return jnp.dot(a, b, preferred_element_type=F32)


def _dot_nt(a, b):
    return lax.dot_general(a, b, (((1,), (1,)), ((), ())), preferred_element_type=F32)


def _split3(x):
    hi = x.astype(BF16)
    r1 = x - hi.astype(F32)
    mid = r1.astype(BF16)
    lo = (r1 - mid.astype(F32)).astype(BF16)
    return hi, mid, lo


def _segsum(x, ones_bd):
    hi, mid, lo = _split3(x)
    return _dot(hi, ones_bd) + _dot(mid, ones_bd) + _dot(lo, ones_bd)


def _layer_norm_rows(z, g, b):
    mu = jnp.mean(z, axis=-1, keepdims=True)
    d = z - mu
    var = jnp.mean(d * d, axis=-1, keepdims=True)
    return d * lax.rsqrt(var + LN_EPS) * g + b


def _rms_norm_rows(z, g):
    return z * lax.rsqrt(jnp.mean(z * z, axis=-1, keepdims=True) + RMS_EPS) * g


def _sigmoid(z):
    return 1.0 / (1.0 + jnp.exp(-z))


def _full(shape):
    nd = len(shape)
    return pl.BlockSpec(shape, lambda *_: (0,) * nd)


def _proj_kernel(x_ref, cos_ref, sin_ref, wq_ref, wkv_ref, wkr_ref, wkrr_ref, wcv_ref, wrw_ref,
                 gq_ref, gkv_ref, wuqn_ref, wuqr_ref, wuqrr_ref, wukp_ref,
                 q_ref, kvb_ref, ckv_ref, kr_ref, u_ref, rc_ref):
    xb = x_ref[...].astype(BF16)
    cos = cos_ref[...]
    sin = sin_ref[...]
    cq = _rms_norm_rows(_dot(xb, wq_ref[...]), gq_ref[...]).astype(BF16)
    qn = _dot(cq, wuqn_ref[...]).astype(BF16)
    qr = (_dot(cq, wuqr_ref[...]) * cos + _dot(cq, wuqrr_ref[...]) * sin) * Q_SCALE
    lane = lax.broadcasted_iota(jnp.int32, qr.shape, 1)
    for h in range(MLA_HEADS):
        slab = qn[:, (h // 2) * LANES:(h // 2 + 1) * LANES]
        q_ref[h, :, 0:KV_LORA] = (_dot(slab, wukp_ref[h]) * Q_SCALE).astype(BF16)
        own = (lane >= h * MLA_ROPE) & (lane < (h + 1) * MLA_ROPE)
        q_ref[h, :, KV_LORA:QK_WIDTH] = jnp.where(own, qr, 0.0).astype(BF16)
    ckv = _rms_norm_rows(_dot(xb, wkv_ref[...]), gkv_ref[...])
    ckv_ref[...] = ckv
    kvb_ref[:, 0:KV_LORA] = ckv.astype(BF16)
    kr8 = _dot(xb, wkr_ref[...]) * cos + _dot(xb, wkrr_ref[...]) * sin
    kvb_ref[:, KV_LORA:QK_WIDTH] = kr8.astype(BF16)
    kr_ref[...] = kr8[:, 0:MLA_ROPE]
    cv = _dot(xb, wcv_ref[...])
    u_ref[...] = cv[:, 0:CONV_CH] * _sigmoid(cv[:, CONV_CH:2 * CONV_CH])
    rc_ref[...] = _dot(xb, wrw_ref[...])


def _proj(x, cos_t, sin_t, w, T):
    n = x.shape[0]
    tm = _tile(n, ROW_TILE)
    if T % tm == 0:
        nt = T // tm
        tab_map = lambda i: (i % nt, 0)
    else:
        assert tm % T == 0
        cos_t = jnp.tile(cos_t, (tm // T, 1))
        sin_t = jnp.tile(sin_t, (tm // T, 1))
        tab_map = lambda i: (0, 0)
    row = lambda c: pl.BlockSpec((tm, c), lambda i: (i, 0))
    wts = [w['wq'], w['wkv'], w['wkr8'], w['wkrr8'], w['wcv'], w['wrw'], w['gq'], w['gkv'],
           w['wuqn'], w['wuqr'], w['wuqrr'], w['wukp']]
    return pl.pallas_call(
        _proj_kernel,
        grid=(n // tm,),
        in_specs=[row(D_MODEL), pl.BlockSpec((tm, KV_LORA), tab_map), pl.BlockSpec((tm, KV_LORA), tab_map)]
                 + [_full(a.shape) for a in wts],
        out_specs=[pl.BlockSpec((MLA_HEADS, tm, QK_WIDTH), lambda i: (0, i, 0)),
                   row(QK_WIDTH), row(KV_LORA), row(MLA_ROPE), row(CONV_CH), row(RWKV_COLS)],
        out_shape=[jax.ShapeDtypeStruct((MLA_HEADS, n, QK_WIDTH), BF16),
                   jax.ShapeDtypeStruct((n, QK_WIDTH), BF16),
                   jax.ShapeDtypeStruct((n, KV_LORA), F32),
                   jax.ShapeDtypeStruct((n, MLA_ROPE), F32),
                   jax.ShapeDtypeStruct((n, CONV_CH), F32),
                   jax.ShapeDtypeStruct((n, RWKV_COLS), F32)],
        compiler_params=_params("parallel"),
        name="proj",
    )(x, cos_t, sin_t, *wts)


def _attn_prompt_kernel(q_ref, kv_ref, o_ref, s_ref, p_ref, m_ref, l_ref, a_ref, acc_ref, *, tq, tk, rc):
    qi = pl.program_id(1)
    rows = MLA_HEADS * tq
    q = q_ref[...].reshape(rows, QK_WIDTH)
    m_ref[...] = jnp.full(m_ref.shape, -jnp.inf, F32)
    l_ref[...] = jnp.zeros(l_ref.shape, F32)
    acc_ref[...] = jnp.zeros(acc_ref.shape, F32)

    half = rows // 2

    def block(k0, width, masked):
        kv = kv_ref[pl.ds(k0, width), :]
        s_ref[0:half, 0:width] = _dot_nt(q[0:half], kv)
        s_ref[half:rows, 0:width] = _dot_nt(q[half:rows], kv)

        def scores(c):
            s = s_ref[pl.ds(c * rc, rc), 0:width]
            if masked:
                q_pos = qi * tq + ((c * rc + lax.broadcasted_iota(jnp.int32, (rc, 1), 0)) & (tq - 1))
                k_pos = k0 + lax.broadcasted_iota(jnp.int32, (1, width), 1)
                s = jnp.where(k_pos <= q_pos, s, -jnp.inf)
            return [s[:, g * LANES:(g + 1) * LANES] for g in range(width // LANES)]

        for c in range(rows // rc):
            sl = pl.ds(c * rc, rc)
            mx = functools.reduce(jnp.maximum, scores(c))
            m_prev = m_ref[sl, :]
            m_new = jnp.maximum(m_prev, jnp.max(mx, axis=-1, keepdims=True))
            a_ref[sl, :] = jnp.exp2(m_prev - m_new)
            m_ref[sl, :] = m_new
        for c in range(rows // rc):
            sl = pl.ds(c * rc, rc)
            m_cur = m_ref[sl, :]
            ps = [jnp.exp2(g - m_cur) for g in scores(c)]
            alpha = a_ref[sl, :]
            l_ref[sl, :] = alpha * l_ref[sl, :] + functools.reduce(jnp.add, ps)
            p_ref[sl, 0:width] = jnp.concatenate(ps, axis=-1).astype(BF16)
            acc_ref[sl, :] = acc_ref[sl, :] * jnp.concatenate([alpha] * (KV_LORA // LANES), axis=-1)
        vals = kv[:, 0:KV_LORA]
        acc_ref[0:half, :] += _dot(p_ref[0:half, 0:width], vals)
        acc_ref[half:rows, :] += _dot(p_ref[half:rows, 0:width], vals)

    q0 = qi * tq
    n_full = q0 // tk

    def full_block(kb, carry):
        block(pl.multiple_of(kb * tk, tk), tk, False)
        return carry

    lax.fori_loop(0, n_full, full_block, 0)
    lead = (q0 - n_full * tk) // tq
    for r in range(tk // tq):
        @pl.when(lead == r)
        def _(r=r):
            block(pl.multiple_of(n_full * tk, tk), (r + 1) * tq, True)

    o = acc_ref[...] / jnp.sum(l_ref[...], axis=-1, keepdims=True)
    o_ref[...] = o.reshape(MLA_HEADS, tq, KV_LORA).astype(BF16)


def _attn_prompt(q, kvb, B, T):
    n = B * T
    tq = _tile(T, ATTN_TQ, 16)
    tk = _tile(T, ATTN_TK, 16)
    assert tq & (tq - 1) == 0 and tk % tq == 0
    nq = T // tq
    rows = MLA_HEADS * tq
    rc = _tile(rows, ATTN_ROW_CHUNK, 16)
    return pl.pallas_call(
        functools.partial(_attn_prompt_kernel, tq=tq, tk=tk, rc=rc),
        grid=(B, nq),
        in_specs=[pl.BlockSpec((MLA_HEADS, tq, QK_WIDTH), lambda b, i: (0, b * nq + i, 0)),
                  pl.BlockSpec((T, QK_WIDTH), lambda b, i: (b, 0))],
        out_specs=pl.BlockSpec((MLA_HEADS, tq, KV_LORA), lambda b, i: (0, b * nq + i, 0)),
        out_shape=jax.ShapeDtypeStruct((MLA_HEADS, n, KV_LORA), BF16),
        scratch_shapes=[pltpu.VMEM((rows, tk), F32), pltpu.VMEM((rows, tk), BF16),
                        pltpu.VMEM((rows, LANES), F32), pltpu.VMEM((rows, LANES), F32),
                        pltpu.VMEM((rows, LANES), F32), pltpu.VMEM((rows, KV_LORA), F32)],
        compiler_params=_params("parallel", "parallel"),
        name="attn_prompt",
    )(q, kvb)


def _attn_sample_kernel(pt_ref, ql_ref, qr_ref, *refs, n_pg, n_new):
    del pt_ref
    c_pages = refs[:n_pg]
    r_pages = refs[n_pg:2 * n_pg]
    cn_ref, rn_ref, o_ref, m_ref, l_ref, acc_ref = refs[2 * n_pg:]
    g = pl.program_id(1)

    @pl.when(g == 0)
    def _():
        m_ref[...] = jnp.full(m_ref.shape, -jnp.inf, F32)
        l_ref[...] = jnp.zeros(l_ref.shape, F32)
        acc_ref[...] = jnp.zeros(acc_ref.shape, F32)

    ql = ql_ref[0]
    qr = qr_ref[0]

    def update(s, vals):
        m_prev = m_ref[...]
        m_new = jnp.maximum(m_prev, jnp.max(s, axis=-1, keepdims=True))
        p = jnp.exp2(s - m_new)
        alpha = jnp.exp2(m_prev - m_new)
        l_ref[...] = alpha * l_ref[...] + jnp.sum(p, axis=-1, keepdims=True)
        acc_ref[...] = alpha * acc_ref[...] + _dot(p.astype(BF16), vals)
        m_ref[...] = m_new

    cs = [c_pages[k][0, 0].astype(BF16) for k in range(n_pg)]
    s = jnp.concatenate([_dot_nt(ql, cs[k]) + _dot(qr, r_pages[k][0, 0].astype(BF16))
                         for k in range(n_pg)], axis=-1)
    update(s, jnp.concatenate(cs, axis=0))

    @pl.when(g == pl.num_programs(1) - 1)
    def _():
        cn = cn_ref[0].astype(BF16)
        s_new = _dot_nt(ql, cn) + _dot_nt(qr, rn_ref[0].astype(BF16))
        t_q = lax.broadcasted_iota(jnp.int32, s_new.shape, 0) & (n_new - 1)
        t_k = lax.broadcasted_iota(jnp.int32, s_new.shape, 1)
        update(jnp.where(t_k <= t_q, s_new, -jnp.inf), cn)
        o_ref[0] = acc_ref[...] / l_ref[...]


def _attn_sample(ql, qr, ckv_new, kr_new, cache_ckv, cache_kr_t, page_table, layer):
    B, rows, _ = ql.shape
    n_new = ckv_new.shape[1]
    assert n_new & (n_new - 1) == 0
    n_pad = max(NEW_ROWS_PAD, n_new)
    ckv_new = jnp.pad(ckv_new, ((0, 0), (0, n_pad - n_new), (0, 0)))
    kr_new = jnp.pad(kr_new, ((0, 0), (0, n_pad - n_new), (0, 0)))
    n_pages = page_table.shape[1]
    n_pg = _tile(n_pages, PAGES_PER_STEP, 1)
    n_g = n_pages // n_pg

    def page_spec(shape, k):
        return pl.BlockSpec((1, 1) + shape, lambda b, g, pt: (layer, pt[b * n_pages + g * n_pg + k], 0, 0))

    grid_spec = pltpu.PrefetchScalarGridSpec(
        num_scalar_prefetch=1,
        grid=(B, n_g),
        in_specs=[pl.BlockSpec((1, rows, KV_LORA), lambda b, g, pt: (b, 0, 0)),
                  pl.BlockSpec((1, rows, MLA_ROPE), lambda b, g, pt: (b, 0, 0))]
                 + [page_spec((PAGE_SIZE, KV_LORA), k) for k in range(n_pg)]
                 + [page_spec((MLA_ROPE, PAGE_SIZE), k) for k in range(n_pg)]
                 + [pl.BlockSpec((1, n_pad, KV_LORA), lambda b, g, pt: (b, 0, 0)),
                    pl.BlockSpec((1, n_pad, MLA_ROPE), lambda b, g, pt: (b, 0, 0))],
        out_specs=pl.BlockSpec((1, rows, KV_LORA), lambda b, g, pt: (b, 0, 0)),
        scratch_shapes=[pltpu.VMEM((rows, 1), F32), pltpu.VMEM((rows, 1), F32),
                        pltpu.VMEM((rows, KV_LORA), F32)],
    )
    return pl.pallas_call(
        functools.partial(_attn_sample_kernel, n_pg=n_pg, n_new=n_new),
        grid_spec=grid_spec,
        out_shape=jax.ShapeDtypeStruct((B, rows, KV_LORA), F32),
        compiler_params=_params("parallel", "arbitrary"),
        name="attn_sample",
    )(page_table.reshape(-1), ql, qr, *([cache_ckv] * n_pg), *([cache_kr_t] * n_pg), ckv_new, kr_new)


def _uvnorm_kernel(o_ref, wuvp_ref, g_ref, out_ref):
    parts = []
    for p in range(MLA_HEADS // 2):
        pair = jnp.concatenate([o_ref[2 * p], o_ref[2 * p + 1]], axis=-1)
        parts.append(_dot(pair, wuvp_ref[p]))
    a = jnp.concatenate(parts, axis=-1)
    out_ref[...] = _rms_norm_rows(a, g_ref[...]).astype(BF16)


def _uvnorm(o, wuvp, g):
    n = o.shape[1]
    tm = _tile(n, ROW_TILE)
    return pl.pallas_call(
        _uvnorm_kernel,
        grid=(n // tm,),
        in_specs=[pl.BlockSpec((MLA_HEADS, tm, KV_LORA), lambda i: (0, i, 0)), _full(wuvp.shape), _full(g.shape)],
        out_specs=pl.BlockSpec((tm, MLA_WIDTH), lambda i: (i, 0)),
        out_shape=jax.ShapeDtypeStruct((n, MLA_WIDTH), BF16),
        compiler_params=_params("parallel"),
        name="uv_norm",
    )(o, wuvp, g)


CONV_PAD = 32


def _conv_kernel(u_ref, buf_ref, w_ref, b_ref, g_ref, bb_ref, y_ref, new_ref, up_ref, sh_ref, *, T, tt):
    lead = CONV_PAD - (CONV_W - 1)
    up_ref[lead:CONV_PAD, :] = buf_ref[0]
    up_ref[CONV_PAD:CONV_PAD + T, :] = u_ref[0]
    for t0 in range(0, T, tt):
        acc = jnp.zeros((tt, CONV_CH), F32) + b_ref[...]
        for b in range(min(SUBLANES, CONV_W)):
            taps = range(b, CONV_W, SUBLANES)
            rows = taps[-1] - b + tt
            sh_ref[0:rows, :] = up_ref[lead + t0 + b:lead + t0 + b + rows, :]
            for k in taps:
                acc = acc + sh_ref[k - b:k - b + tt, :] * w_ref[k:k + 1, :]
        z = _layer_norm_rows(acc, g_ref[...], bb_ref[...])
        y_ref[0, t0:t0 + tt, :] = z * _sigmoid(z)
    new_ref[0] = up_ref[lead + T:CONV_PAD + T, :]


def _conv(u, buf, w, b, g, bb):
    B, T, _ = u.shape
    tt = _tile(T, 256)
    return pl.pallas_call(
        functools.partial(_conv_kernel, T=T, tt=tt),
        grid=(B,),
        in_specs=[pl.BlockSpec((1, T, CONV_CH), lambda i: (i, 0, 0)),
                  pl.BlockSpec((1, CONV_W - 1, CONV_CH), lambda i: (i, 0, 0)),
                  _full(w.shape), _full(b.shape), _full(g.shape), _full(bb.shape)],
        out_specs=[pl.BlockSpec((1, T, CONV_CH), lambda i: (i, 0, 0)),
                   pl.BlockSpec((1, CONV_W - 1, CONV_CH), lambda i: (i, 0, 0))],
        out_shape=[jax.ShapeDtypeStruct((B, T, CONV_CH), F32),
                   jax.ShapeDtypeStruct((B, CONV_W - 1, CONV_CH), F32)],
        scratch_shapes=[pltpu.VMEM((CONV_PAD + T, CONV_CH), F32),
                        pltpu.VMEM((tt + CONV_PAD, CONV_CH), F32)],
        compiler_params=_params("parallel"),
        name="conv_module",
    )(u, buf, w, b, g, bb)


def _rwkv_prep_kernel(c_ref, s_ref, mu_ref, w0_ref, a0_ref, kk_ref, ka_ref, rk_ref,
                      w2_ref, a2_ref, g2_ref, ones_ref,
                      r_out, w_out, nkk_out, kka_out, k_out, v_out, g_out, bonus_out, *, feature_major):
    def put(ref, val):
        ref[...] = val.T if feature_major else val

    cols = c_ref[...]
    xm = cols + (s_ref[...] - cols) * mu_ref[...]
    W = RWKV_WIDTH
    r = xm[:, 0:W]
    k = xm[:, W:2 * W]
    v = xm[:, 2 * W:3 * W]
    tail = xm[:, 3 * W:3 * W + LANES]
    ones_bd = ones_ref[...]
    z = w0_ref[...] + _dot(jnp.tanh(tail).astype(BF16), w2_ref[...])
    softplus = jnp.maximum(-z, 0.0) + jnp.log(1.0 + jnp.exp(-jnp.abs(z)))
    decay = jnp.exp(-jnp.exp(-softplus - 0.5))
    a = _sigmoid(a0_ref[...] + _dot(tail.astype(BF16), a2_ref[...]))
    g = _dot(_sigmoid(tail).astype(BF16), g2_ref[...])
    kk = k * kk_ref[...]
    norm = jnp.sqrt(_segsum(kk * kk, ones_bd))
    kk = kk / jnp.maximum(norm, 1e-12)
    kp = k * (1.0 + (a - 1.0) * ka_ref[...])
    put(r_out, r)
    put(w_out, decay)
    put(nkk_out, -kk)
    put(kka_out, kk * a)
    put(k_out, kp)
    put(v_out, v)
    g_out[...] = g
    bonus_out[...] = _segsum(r * kp * rk_ref[...], ones_bd) * v


def _rwkv_prep(cols, shifted, w, B, T):
    n = cols.shape[0]
    tm = _tile(n, ROW_TILE)
    feature_major = T % tm == 0 and tm % LANES == 0
    row = lambda c: pl.BlockSpec((tm, c), lambda i: (i, 0))
    wts = [w['rwkv_mu'], w['rwkv_w0'], w['rwkv_a0'], w['rwkv_k_k'], w['rwkv_k_a'], w['rwkv_r_k'],
           w['w2p'], w['a2p'], w['g2p'], w['ones_bd']]
    if feature_major:
        nt = T // tm
        op_spec = pl.BlockSpec((RWKV_WIDTH, tm), lambda i: (i // nt, i % nt))
        op_shape = jax.ShapeDtypeStruct((B * RWKV_WIDTH, T), F32)
    else:
        op_spec = row(RWKV_WIDTH)
        op_shape = jax.ShapeDtypeStruct((n, RWKV_WIDTH), F32)
    outs = pl.pallas_call(
        functools.partial(_rwkv_prep_kernel, feature_major=feature_major),
        grid=(n // tm,),
        in_specs=[row(RWKV_COLS), row(RWKV_COLS)] + [_full(a.shape) for a in wts],
        out_specs=[op_spec] * 6 + [row(RWKV_WIDTH)] * 2,
        out_shape=[op_shape] * 6 + [jax.ShapeDtypeStruct((n, RWKV_WIDTH), F32)] * 2,
        compiler_params=_params("parallel"),
        name="rwkv_prep",
    )(cols, shifted, *wts)
    return outs, feature_major


def _sublane_allsum(x):
    x = x + pltpu.roll(x, 4, 0)
    x = x + pltpu.roll(x, 2, 0)
    return x + pltpu.roll(x, 1, 0)


def _scan_kernel(w_ref, nkk_ref, kka_ref, k_ref, r_ref, v_ref, s0_ref, y_ref, s_ref, *, n_steps, n_ih, unroll):
    @pl.when(pl.program_id(0) == 0)
    def _():
        s_ref[...] = s0_ref[...]

    nv = RWKV_HEAD // SUBLANES
    shape3 = (nv, SUBLANES, LANES)

    def key_tile(ref, t):
        return ref[t].reshape(shape3)

    def step(t, carry):
        def group(gi, c2):
            for u in range(unroll):
                ih = gi * unroll + u
                s = s_ref[ih].reshape(shape3)
                sa = _sublane_allsum(jnp.sum(s * key_tile(nkk_ref, t), axis=0))
                v_row = v_ref[ih, pl.ds(t, 1), :]
                s = s * key_tile(w_ref, t) + key_tile(kka_ref, t) * sa[None] + key_tile(k_ref, t) * v_row[None]
                s_ref[ih] = s.reshape(RWKV_HEAD, LANES)
                y = _sublane_allsum(jnp.sum(s * key_tile(r_ref, t), axis=0))
                y_ref[ih, pl.ds(t, 1), :] = y[0:1, :]
            return c2

        lax.fori_loop(0, n_ih // unroll, group, 0)
        return carry

    lax.fori_loop(0, n_steps, step, 0)


def _scan(ops, v_rows, s0):
    n_ih, T, _ = v_rows.shape
    tc = _tile(T, SCAN_STEPS, SUBLANES)
    unroll = _tile(n_ih, 16, 1)
    key_spec = pl.BlockSpec((tc, RWKV_HEAD, LANES), lambda i: (i, 0, 0))
    val_spec = pl.BlockSpec((n_ih, tc, LANES), lambda i: (0, i, 0))
    return pl.pallas_call(
        functools.partial(_scan_kernel, n_steps=tc, n_ih=n_ih, unroll=unroll),
        grid=(T // tc,),
        in_specs=[key_spec] * 5 + [val_spec, _full(s0.shape)],
        out_specs=[val_spec, _full(s0.shape)],
        out_shape=[jax.ShapeDtypeStruct((n_ih, T, LANES), F32), jax.ShapeDtypeStruct(s0.shape, F32)],
        compiler_params=_params("arbitrary"),
        name="rwkv_scan",
    )(*ops, v_rows, s0)


def _scan_in_kernel(src_ref, dst_ref, *, chains, i_lo, key_operand):
    def rows_of(idx):
        return src_ref[pl.ds(idx, chains, stride=RWKV_HEAD), :]

    if key_operand:
        for j in range(RWKV_HEAD):
            dst_ref[:, j, :] = jnp.concatenate([rows_of(j)] * i_lo, axis=0).T
    else:
        for ih in range(RWKV_HEAD // i_lo):
            dst_ref[ih] = jnp.concatenate([rows_of(ih * i_lo + q) for q in range(i_lo)], axis=0).T


def _scan_in(x, B, T, i_lo, key_operand):
    chains = B * RWKV_HEADS
    n_ih = RWKV_HEAD // i_lo
    tt = LANES
    if key_operand:
        out_spec = pl.BlockSpec((tt, RWKV_HEAD, LANES), lambda i: (i, 0, 0))
        out_shape = jax.ShapeDtypeStruct((T, RWKV_HEAD, LANES), F32)
    else:
        out_spec = pl.BlockSpec((n_ih, tt, LANES), lambda i: (0, i, 0))
        out_shape = jax.ShapeDtypeStruct((n_ih, T, LANES), F32)
    return pl.pallas_call(
        functools.partial(_scan_in_kernel, chains=chains, i_lo=i_lo, key_operand=key_operand),
        grid=(T // tt,),
        in_specs=[pl.BlockSpec((chains * RWKV_HEAD, tt), lambda i: (0, i))],
        out_specs=out_spec,
        out_shape=out_shape,
        compiler_params=_params("parallel"),
        name="rwkv_scan_in",
    )(x)


def _scan_out_kernel(y_ref, out_ref, *, chains, i_lo):
    for ih in range(RWKV_HEAD // i_lo):
        tile = y_ref[ih].T
        for q in range(i_lo):
            out_ref[pl.ds(ih * i_lo + q, chains, stride=RWKV_HEAD), :] = tile[q * chains:(q + 1) * chains, :]


def _scan_out(y_rows, B, T, i_lo):
    chains = B * RWKV_HEADS
    n_ih = RWKV_HEAD // i_lo
    tt = LANES
    return pl.pallas_call(
        functools.partial(_scan_out_kernel, chains=chains, i_lo=i_lo),
        grid=(T // tt,),
        in_specs=[pl.BlockSpec((n_ih, tt, LANES), lambda i: (0, i, 0))],
        out_specs=pl.BlockSpec((chains * RWKV_HEAD, tt), lambda i: (0, i)),
        out_shape=jax.ShapeDtypeStruct((chains * RWKV_HEAD, T), F32),
        compiler_params=_params("parallel"),
        name="rwkv_scan_out",
    )(y_rows)


def _to_key_tiles(x, B, T, i_lo):
    x = x.reshape(B, T, RWKV_HEADS, RWKV_HEAD).transpose(1, 3, 0, 2).reshape(T, RWKV_HEAD, B * RWKV_HEADS)
    return jnp.tile(x, (1, 1, i_lo))


def _to_val_rows(x, B, T, i_lo):
    n_ih = RWKV_HEAD // i_lo
    x = x.reshape(B, T, RWKV_HEADS, n_ih, i_lo).transpose(3, 1, 4, 0, 2)
    return x.reshape(n_ih, T, i_lo * B * RWKV_HEADS)


def _from_val_rows(y, B, T, i_lo):
    n_ih = RWKV_HEAD // i_lo
    y = y.reshape(n_ih, T, i_lo, B, RWKV_HEADS).transpose(3, 1, 4, 0, 2)
    return y.reshape(B * T, RWKV_WIDTH)


def _state_to_tiles(s, i_lo):
    B = s.shape[0]
    n_ih = RWKV_HEAD // i_lo
    s = s.reshape(B, RWKV_HEADS, n_ih, i_lo, RWKV_HEAD).transpose(2, 4, 3, 0, 1)
    return s.reshape(n_ih, RWKV_HEAD, i_lo * B * RWKV_HEADS)


def _state_from_tiles(s, B, i_lo):
    n_ih = RWKV_HEAD // i_lo
    s = s.reshape(n_ih, RWKV_HEAD, i_lo, B, RWKV_HEADS).transpose(3, 4, 0, 2, 1)
    return s.reshape(B, RWKV_HEADS, RWKV_HEAD, RWKV_HEAD)


def _outproj_kernel(x_ref, at_ref, cv_ref, y_ref, bonus_ref, g_ref, lg_ref, lb_ref, ones_ref,
                    wa_ref, wc_ref, wr_ref, n1g_ref, n1b_ref, out_ref, *, alpha, y_feature_major):
    ones_bd = ones_ref[...]
    y = y_ref[...].T if y_feature_major else y_ref[...]
    inv = 1.0 / RWKV_HEAD
    mu = _segsum(y, ones_bd) * inv
    d = y - mu
    var = _segsum(d * d, ones_bd) * inv
    yn = d * lax.rsqrt(var + LNX_EPS) * lg_ref[...] + lb_ref[...]
    rw = ((yn + bonus_ref[...]) * g_ref[...]).astype(BF16)
    h = _dot(at_ref[...], wa_ref[...]) + _dot(cv_ref[...].astype(BF16), wc_ref[...]) + _dot(rw, wr_ref[...])
    out_ref[...] = _layer_norm_rows(alpha * x_ref[...] + h, n1g_ref[...], n1b_ref[...])


def _outproj(x, attn, conv, y, bonus, g, w, alpha, T, y_feature_major):
    n = x.shape[0]
    tm = _tile(n, ROW_TILE)
    row = lambda c: pl.BlockSpec((tm, c), lambda i: (i, 0))
    wts = [w['rwkv_lnx_g'], w['rwkv_lnx_b'], w['ones_bd'], w['wo_a'], w['wo_c'], w['wo_r'], w['ln1_g'], w['ln1_b']]
    if y_feature_major:
        nt = T // tm
        y_spec = pl.BlockSpec((RWKV_WIDTH, tm), lambda i: (i // nt, i % nt))
    else:
        y_spec = row(RWKV_WIDTH)
    return pl.pallas_call(
        functools.partial(_outproj_kernel, alpha=alpha, y_feature_major=y_feature_major),
        grid=(n // tm,),
        in_specs=[row(D_MODEL), row(MLA_WIDTH), row(CONV_CH), y_spec, row(RWKV_WIDTH), row(RWKV_WIDTH)]
                 + [_full(a.shape) for a in wts],
        out_specs=row(D_MODEL),
        out_shape=jax.ShapeDtypeStruct((n, D_MODEL), F32),
        compiler_params=_params("parallel"),
        name="out_proj",
    )(x, attn, conv, y, bonus, g, *wts)


def _norm_ple(x1, f, p, n2g, n2b, wple, wgate, alpha):
    x2 = _layer_norm_rows(alpha * x1 + f, n2g, n2b)
    gate = _sigmoid(_dot(x2.astype(BF16), wgate))
    return x2 + gate * _dot(p.astype(BF16), wple)


def _ffn_kernel(x_ref, p_ref, wg_ref, wu_ref, wd_ref, n2g_ref, n2b_ref, wple_ref, wgate_ref,
                out_ref, xb_ref, acc_ref, *, alpha):
    f = pl.program_id(1)

    @pl.when(f == 0)
    def _():
        xb_ref[...] = x_ref[...].astype(BF16)
        acc_ref[...] = jnp.zeros(acc_ref.shape, F32)

    xb = xb_ref[...]
    gate = _dot(xb, wg_ref[...])
    hid = (gate * _sigmoid(gate) * _dot(xb, wu_ref[...])).astype(BF16)
    acc_ref[...] += _dot(hid, wd_ref[...])

    @pl.when(f == pl.num_programs(1) - 1)
    def _():
        out_ref[...] = _norm_ple(x_ref[...], acc_ref[...], p_ref[...], n2g_ref[...], n2b_ref[...],
                                 wple_ref[...], wgate_ref[...], alpha)


def _ffn(x, p, wg, wu, wd, w, alpha):
    n = x.shape[0]
    d_ff = wg.shape[1]
    tm = _tile(n, ROW_TILE)
    tf = _tile(d_ff, 1408, LANES)
    row = lambda c: pl.BlockSpec((tm, c), lambda i, f: (i, 0))
    wts = [w['ln2_g'], w['ln2_b'], w['ple_w'], w['ple_gate_w']]
    return pl.pallas_call(
        functools.partial(_ffn_kernel, alpha=alpha),
        grid=(n // tm, d_ff // tf),
        in_specs=[row(D_MODEL), row(p.shape[1]),
                  pl.BlockSpec((D_MODEL, tf), lambda i, f: (0, f)),
                  pl.BlockSpec((D_MODEL, tf), lambda i, f: (0, f)),
                  pl.BlockSpec((tf, D_MODEL), lambda i, f: (f, 0))] + [_full(a.shape) for a in wts],
        out_specs=row(D_MODEL),
        out_shape=jax.ShapeDtypeStruct((n, D_MODEL), F32),
        scratch_shapes=[pltpu.VMEM((tm, D_MODEL), BF16), pltpu.VMEM((tm, D_MODEL), F32)],
        compiler_params=_params("parallel", "arbitrary"),
        name="ffn_dense",
    )(x, p, wg, wu, wd, *wts)


def _router_kernel(x_ref, wr_ref, idx_ref, gate_ref):
    lt = _dot_nt(wr_ref[...].astype(BF16), x_ref[...].astype(BF16))
    e = lax.broadcasted_iota(jnp.int32, lt.shape, 0).astype(F32)
    lt = jnp.where(e < N_EXPERTS, lt, -jnp.inf)
    m1 = jnp.max(lt, axis=0, keepdims=True)
    i1 = jnp.min(jnp.where(lt == m1, e, float(N_EXPERTS)), axis=0, keepdims=True)
    lt2 = jnp.where(e == i1, -jnp.inf, lt)
    m2 = jnp.max(lt2, axis=0, keepdims=True)
    i2 = jnp.min(jnp.where(lt2 == m2, e, float(N_EXPERTS)), axis=0, keepdims=True)
    d = jnp.exp(m2 - m1)
    idx_ref[0:1, :] = i1.astype(jnp.int32)
    idx_ref[1:2, :] = i2.astype(jnp.int32)
    gate_ref[0:1, :] = 1.0 / (1.0 + d)
    gate_ref[1:2, :] = d / (1.0 + d)


def _router(x, wr_t):
    n = x.shape[0]
    tm = _tile(n, ROW_TILE, LANES)
    return pl.pallas_call(
        _router_kernel,
        grid=(n // tm,),
        in_specs=[pl.BlockSpec((tm, D_MODEL), lambda i: (i, 0)), _full(wr_t.shape)],
        out_specs=[pl.BlockSpec((TOP_K, tm), lambda i: (0, i)), pl.BlockSpec((TOP_K, tm), lambda i: (0, i))],
        out_shape=[jax.ShapeDtypeStruct((TOP_K, n), jnp.int32), jax.ShapeDtypeStruct((TOP_K, n), F32)],
        compiler_params=_params("parallel"),
        name="moe_router",
    )(x, wr_t)


def _row_copy(src_hbm, buf_ref, sem, r, src_row):
    return pltpu.make_async_copy(src_hbm.at[pl.ds(src_row, 1)], buf_ref.at[pl.ds(r, 1)], sem)


def _gather_start(idx_ref, base, src_hbm, buf_ref, sem, n_rows):
    def start(r, c):
        _row_copy(src_hbm, buf_ref, sem, r, idx_ref[base + r]).start()
        return c

    lax.fori_loop(0, n_rows, start, 0, unroll=GATHER_UNROLL)


def _gather_wait(src_hbm, buf_ref, sem, n_rows):
    def wait(r, c):
        _row_copy(src_hbm, buf_ref, sem, r, 0).wait()
        return c

    lax.fori_loop(0, n_rows, wait, 0, unroll=GATHER_UNROLL)


def _moe_gather_kernel(tok_ref, used_ref, x_hbm, out_ref, buf_ref, sem, *, rows):
    base = pl.program_id(0) * rows

    @pl.when(base < used_ref[0])
    def _():
        _gather_start(tok_ref, base, x_hbm, buf_ref, sem, rows)
        _gather_wait(x_hbm, buf_ref, sem, rows)
        out_ref[...] = buf_ref[...].astype(BF16)

    @pl.when(base >= used_ref[0])
    def _():
        out_ref[...] = jnp.zeros(out_ref.shape, BF16)


def _moe_gather(row_tok, used_rows, x):
    n_rows = row_tok.shape[0]
    rows = _tile(n_rows, GATHER_ROWS, 16)
    grid_spec = pltpu.PrefetchScalarGridSpec(
        num_scalar_prefetch=2,
        grid=(n_rows // rows,),
        in_specs=[pl.BlockSpec(memory_space=pl.ANY)],
        out_specs=pl.BlockSpec((rows, D_MODEL), lambda i, tok, used: (i, 0)),
        scratch_shapes=[pltpu.VMEM((rows, D_MODEL), F32), pltpu.SemaphoreType.DMA],
    )
    return pl.pallas_call(
        functools.partial(_moe_gather_kernel, rows=rows),
        grid_spec=grid_spec,
        out_shape=jax.ShapeDtypeStruct((n_rows, D_MODEL), BF16),
        compiler_params=_params("arbitrary"),
        name="moe_gather",
    )(row_tok, used_rows, x)


def _moe_ffn_kernel(be_ref, used_ref, x_ref, wg_ref, wu_ref, wd_ref, out_ref, acc_ref):
    del be_ref
    i = pl.program_id(0)
    f = pl.program_id(1)
    last_f = pl.num_programs(1) - 1

    @pl.when(i < used_ref[0])
    def _():
        @pl.when(f == 0)
        def _():
            acc_ref[...] = jnp.zeros(acc_ref.shape, F32)

        xb = x_ref[...]
        gate = _dot(xb, wg_ref[0, 0])
        hid = (gate * _sigmoid(gate) * _dot(xb, wu_ref[0, 0])).astype(BF16)
        acc_ref[...] += _dot(hid, wd_ref[0, 0])

        @pl.when(f == last_f)
        def _():
            out_ref[...] = acc_ref[...]

    @pl.when((i >= used_ref[0]) & (f == last_f))
    def _():
        out_ref[...] = jnp.zeros(out_ref.shape, F32)


def _moe_ffn(block_e, used_blocks, xs, wg, wu, wd, j, m_rows):
    n_rows = xs.shape[0]
    d_ff = wg.shape[3]
    tf = _tile(d_ff, MOE_FF_TILE, 2 * LANES)
    n_f = d_ff // tf

    def blk(i, used):
        return jnp.minimum(i, used[0] - 1)

    def ftile(i, f, used):
        return jnp.where(i < used[0], f, n_f - 1)

    grid_spec = pltpu.PrefetchScalarGridSpec(
        num_scalar_prefetch=2,
        grid=(n_rows // m_rows, n_f),
        in_specs=[pl.BlockSpec((m_rows, D_MODEL), lambda i, f, be, used: (blk(i, used), 0)),
                  pl.BlockSpec((1, 1, D_MODEL, tf), lambda i, f, be, used: (j, be[blk(i, used)], 0, ftile(i, f, used))),
                  pl.BlockSpec((1, 1, D_MODEL, tf), lambda i, f, be, used: (j, be[blk(i, used)], 0, ftile(i, f, used))),
                  pl.BlockSpec((1, 1, tf, D_MODEL), lambda i, f, be, used: (j, be[blk(i, used)], ftile(i, f, used), 0))],
        out_specs=pl.BlockSpec((m_rows, D_MODEL), lambda i, f, be, used: (i, 0)),
        scratch_shapes=[pltpu.VMEM((m_rows, D_MODEL), F32)],
    )
    return pl.pallas_call(
        _moe_ffn_kernel,
        grid_spec=grid_spec,
        out_shape=jax.ShapeDtypeStruct((n_rows, D_MODEL), F32),
        compiler_params=_params("parallel", "arbitrary"),
        name="moe_ffn",
    )(block_e, used_blocks, xs, wg, wu, wd)


def _moe_combine_kernel(pos0_ref, pos1_ref, rows_hbm, x_ref, p_ref, gate_ref, n2g_ref, n2b_ref, wple_ref, wgate_ref,
                        out_ref, b0_ref, b1_ref, sem0, sem1, *, tm, alpha):
    base = pl.program_id(0) * tm
    _gather_start(pos0_ref, base, rows_hbm, b0_ref, sem0, tm)
    _gather_start(pos1_ref, base, rows_hbm, b1_ref, sem1, tm)
    _gather_wait(rows_hbm, b0_ref, sem0, tm)
    _gather_wait(rows_hbm, b1_ref, sem1, tm)
    gts = gate_ref[...]
    f = b0_ref[...] * gts[:, 0:1] + b1_ref[...] * gts[:, 1:2]
    out_ref[...] = _norm_ple(x_ref[...], f, p_ref[...], n2g_ref[...], n2b_ref[...],
                             wple_ref[...], wgate_ref[...], alpha)


def _moe_combine(pos0, pos1, out_rows, x, p, gates, w, alpha):
    n = x.shape[0]
    tm = _tile(n, GATHER_ROWS)
    row = lambda c: pl.BlockSpec((tm, c), lambda i, a, b: (i, 0))
    wts = [w['ln2_g'], w['ln2_b'], w['ple_w'], w['ple_gate_w']]
    grid_spec = pltpu.PrefetchScalarGridSpec(
        num_scalar_prefetch=2,
        grid=(n // tm,),
        in_specs=[pl.BlockSpec(memory_space=pl.ANY), row(D_MODEL), row(p.shape[1]), row(TOP_K)]
                 + [_full(a.shape) for a in wts],
        out_specs=row(D_MODEL),
        scratch_shapes=[pltpu.VMEM((tm, D_MODEL), F32), pltpu.VMEM((tm, D_MODEL), F32),
                        pltpu.SemaphoreType.DMA, pltpu.SemaphoreType.DMA],
    )
    return pl.pallas_call(
        functools.partial(_moe_combine_kernel, tm=tm, alpha=alpha),
        grid_spec=grid_spec,
        out_shape=jax.ShapeDtypeStruct((n, D_MODEL), F32),
        compiler_params=_params("arbitrary"),
        name="moe_combine",
    )(pos0, pos1, out_rows, x, p, gates, *wts)


def _moe(x, p, wr_t, wg, wu, wd, j, w, alpha):
    n = x.shape[0]
    idx, gates = _router(x, wr_t)
    flat_e = idx.T.reshape(-1)
    onehot = (flat_e[:, None] == jnp.arange(N_EXPERTS, dtype=jnp.int32)[None, :]).astype(jnp.int32)
    csum = jnp.cumsum(onehot, axis=0)
    rank = jnp.take_along_axis(csum, flat_e[:, None], axis=1)[:, 0] - 1
    counts = csum[-1]
    m_rows = min(MOE_ROWS, max(SUBLANES * 2, n))
    padded = (counts + m_rows - 1) // m_rows * m_rows
    pad_end = jnp.cumsum(padded)
    dest = (pad_end - padded)[flat_e] + rank
    n_assign = n * TOP_K
    n_blocks = -(-n_assign // m_rows) + N_EXPERTS
    flat_tok = jnp.repeat(jnp.arange(n, dtype=jnp.int32), TOP_K)
    row_tok = jnp.zeros((n_blocks * m_rows,), jnp.int32).at[dest].set(flat_tok)
    block_e = jnp.minimum(jnp.searchsorted(pad_end, jnp.arange(n_blocks, dtype=jnp.int32) * m_rows, side='right'),
                          N_EXPERTS - 1).astype(jnp.int32)
    used_rows = pad_end[-1:].astype(jnp.int32)
    xs = _moe_gather(row_tok, used_rows, x)
    out_rows = _moe_ffn(block_e, used_rows // m_rows, xs, wg, wu, wd, j, m_rows)
    dest2 = dest.reshape(n, TOP_K)
    return _moe_combine(dest2[:, 0], dest2[:, 1], out_rows, x, p, gates.T, w, alpha)


def _rot_cols(wmat):
    half = MLA_ROPE // 2
    return jnp.concatenate([-wmat[..., half:], wmat[..., :half]], axis=-1)


def _layer_weights(l, a):
    o1 = Q_LORA
    o2 = o1 + KV_LORA
    o3 = o2 + MLA_ROPE
    o4 = o3 + 2 * CONV_CH
    w_in = a['w_in'][l]
    w = {}
    w['wq'] = w_in[:, :o1].astype(BF16)
    w['wkv'] = w_in[:, o1:o2].astype(BF16)
    wkr = w_in[:, o2:o3]
    w['wkr8'] = jnp.tile(wkr, (1, MLA_HEADS)).astype(BF16)
    w['wkrr8'] = jnp.tile(_rot_cols(wkr), (1, MLA_HEADS)).astype(BF16)
    w['wcv'] = w_in[:, o3:o4].astype(BF16)
    w['wrw'] = w_in[:, o4:].astype(BF16)
    w['gq'] = a['q_norm_g'][l][None, :]
    w['gkv'] = a['kv_norm_g'][l][None, :]
    wuq = a['w_uq'][l].reshape(Q_LORA, MLA_HEADS, MLA_NOPE + MLA_ROPE)
    w['wuqn'] = wuq[:, :, :MLA_NOPE].reshape(Q_LORA, MLA_HEADS * MLA_NOPE).astype(BF16)
    wuqr = wuq[:, :, MLA_NOPE:]
    w['wuqr'] = wuqr.reshape(Q_LORA, MLA_HEADS * MLA_ROPE).astype(BF16)
    w['wuqrr'] = _rot_cols(wuqr).reshape(Q_LORA, MLA_HEADS * MLA_ROPE).astype(BF16)
    wuk_t = a['w_uk'][l].transpose(1, 2, 0)
    zeros = jnp.zeros_like(wuk_t)
    even = jnp.concatenate([wuk_t, zeros], axis=1)
    odd = jnp.concatenate([zeros, wuk_t], axis=1)
    is_odd = (jnp.arange(MLA_HEADS) % 2 == 1)[:, None, None]
    w['wukp'] = jnp.where(is_odd, odd, even).astype(BF16)
    wuv = a['w_uv'][l].transpose(1, 0, 2)
    zv = jnp.zeros((MLA_HEADS // 2, KV_LORA, MLA_V), F32)
    top = jnp.concatenate([wuv[0::2], zv], axis=2)
    bot = jnp.concatenate([zv, wuv[1::2]], axis=2)
    w['wuvp'] = jnp.concatenate([top, bot], axis=1).astype(BF16)
    w['mla_out_g'] = a['mla_out_g'][l][None, :]
    w['conv_w'] = a['conv_w'][l]
    for name in ('conv_b', 'conv_ln_g', 'conv_ln_b', 'rwkv_mu', 'rwkv_w0', 'rwkv_a0', 'rwkv_k_k', 'rwkv_k_a',
                 'rwkv_lnx_g', 'rwkv_lnx_b', 'ln1_g', 'ln1_b', 'ln2_g', 'ln2_b'):
        w[name] = a[name][l][None, :]
    w['rwkv_r_k'] = a['rwkv_r_k'][l].reshape(1, RWKV_WIDTH)
    pad = lambda m, lo: jnp.zeros((LANES, RWKV_WIDTH), F32).at[lo:lo + m.shape[0]].set(m).astype(BF16)
    w['w2p'] = pad(a['rwkv_w2'][l], 0)
    w['a2p'] = pad(a['rwkv_a2'][l], W_LORA)
    w['g2p'] = pad(a['rwkv_g2'][l], W_LORA + A_LORA)
    head_of = jnp.arange(RWKV_WIDTH) // RWKV_HEAD
    w['ones_bd'] = (head_of[:, None] == head_of[None, :]).astype(BF16)
    w_out = a['w_out'][l]
    w['wo_a'] = w_out[:MLA_WIDTH].astype(BF16)
    w['wo_c'] = w_out[MLA_WIDTH:MLA_WIDTH + CONV_CH].astype(BF16)
    w['wo_r'] = w_out[MLA_WIDTH + CONV_CH:].astype(BF16)
    w['ple_w'] = a['ple_w'][l].astype(BF16)
    w['ple_gate_w'] = a['ple_gate_w'][l].astype(BF16)
    return w


def _rope_tables(pos):
    half = MLA_ROPE // 2
    inv = ROPE_BASE ** (-jnp.arange(half, dtype=F32) / half)
    ang = pos.astype(F32)[:, None] * inv[None, :]
    cos = jnp.cos(ang)
    sin = jnp.sin(ang)
    rep = lambda t: jnp.tile(jnp.concatenate([t, t], axis=-1), (1, MLA_HEADS))
    return rep(cos), rep(sin)


def _run_group(x, p_emb, pos, a, layer_w, mixers_w, depth, alpha, past):
    B, T, _ = x.shape
    n = B * T
    x = x.reshape(n, D_MODEL)
    cos_t, sin_t = _rope_tables(pos)
    chains = B * RWKV_HEADS
    assert LANES % chains == 0 and RWKV_HEAD % (LANES // chains) == 0
    i_lo = LANES // chains
    news = []
    for l in range(depth):
        w = layer_w[l]
        q, kvb, ckv, kr, u, rcols = _proj(x, cos_t, sin_t, w, T)
        if past is None:
            o = _attn_prompt(q, kvb, B, T)
            conv_buf = jnp.zeros((B, CONV_W - 1, CONV_CH), F32)
            shift_buf = jnp.zeros((B, RWKV_COLS), F32)
            wkv0 = jnp.zeros((B, RWKV_HEADS, RWKV_HEAD, RWKV_HEAD), F32)
        else:
            rows = MLA_HEADS * T
            q4 = q.reshape(MLA_HEADS, B, T, QK_WIDTH).transpose(1, 0, 2, 3)
            ql = q4[..., :KV_LORA].reshape(B, rows, KV_LORA)
            qr = jnp.stack([q4[:, h, :, KV_LORA + h * MLA_ROPE:KV_LORA + (h + 1) * MLA_ROPE]
                            for h in range(MLA_HEADS)], axis=1).reshape(B, rows, MLA_ROPE)
            o = _attn_sample(ql, qr, ckv.reshape(B, T, KV_LORA), kr.reshape(B, T, MLA_ROPE),
                             past['cache_ckv'], past['cache_kr'], past['page_table'], l)
            o = o.reshape(B, MLA_HEADS, T, KV_LORA).transpose(1, 0, 2, 3).reshape(MLA_HEADS, n, KV_LORA).astype(BF16)
            conv_buf, shift_buf, wkv0 = past['state_conv'][l], past['state_shift'][l], past['state_wkv'][l]
        attn = _uvnorm(o, w['wuvp'], w['mla_out_g'])
        conv, conv_new = _conv(u.reshape(B, T, CONV_CH), conv_buf, w['conv_w'], w['conv_b'],
                               w['conv_ln_g'], w['conv_ln_b'])
        rc3 = rcols.reshape(B, T, RWKV_COLS)
        shifted = jnp.concatenate([shift_buf[:, None, :], rc3[:, :-1]], axis=1).reshape(n, RWKV_COLS)
        (r, dec, nkk, kka, kp, v, g, bonus), feature_major = _rwkv_prep(rcols, shifted, w, B, T)
        if feature_major:
            ops = [_scan_in(t, B, T, i_lo, True) for t in (dec, nkk, kka, kp, r)]
            v_rows = _scan_in(v, B, T, i_lo, False)
        else:
            ops = [_to_key_tiles(t, B, T, i_lo) for t in (dec, nkk, kka, kp, r)]
            v_rows = _to_val_rows(v, B, T, i_lo)
        y_rows, s_fin = _scan(ops, v_rows, _state_to_tiles(wkv0, i_lo))
        y = _scan_out(y_rows, B, T, i_lo) if feature_major else _from_val_rows(y_rows, B, T, i_lo)
        x1 = _outproj(x, attn, conv.reshape(n, CONV_CH), y, bonus, g, w, alpha, T, feature_major)
        j = l // 2
        p_l = p_emb[l].reshape(n, -1)
        if l % 2 == 0:
            x = _ffn(x1, p_l, mixers_w['ffn_g'][j], mixers_w['ffn_u'][j], mixers_w['ffn_d'][j], w, alpha)
        else:
            x = _moe(x1, p_l, mixers_w['router_t'][j], mixers_w['moe_g'], mixers_w['moe_u'],
                     mixers_w['moe_d'], j, w, alpha)
        news.append((ckv.reshape(B, T, KV_LORA), kr.reshape(B, T, MLA_ROPE), conv_new, rc3[:, -1],
                     _state_from_tiles(s_fin, B, i_lo)))
    outs = tuple(jnp.stack([st[i] for st in news]) for i in range(5))
    return (x.reshape(B, T, D_MODEL),) + outs


def kernel(x_prompt, x_sample, cache_ckv, cache_kr, state_conv, state_shift, state_wkv, page_table,
           p_prompt, p_sample, w_in, q_norm_g, w_uq, kv_norm_g, w_uk, w_uv, mla_out_g,
           conv_w, conv_b, conv_ln_g, conv_ln_b, rwkv_mu, rwkv_w0, rwkv_w2, rwkv_a0, rwkv_a2,
           rwkv_g2, rwkv_k_k, rwkv_k_a, rwkv_r_k, rwkv_lnx_g, rwkv_lnx_b, w_out, ln1_g, ln1_b,
           ffn_w_gate, ffn_w_up, ffn_w_down, moe_router, moe_w_gate, moe_w_up, moe_w_down,
           ln2_g, ln2_b, ple_w, ple_gate_w):
    a = dict(w_in=w_in, q_norm_g=q_norm_g, w_uq=w_uq, kv_norm_g=kv_norm_g, w_uk=w_uk, w_uv=w_uv,
             mla_out_g=mla_out_g, conv_w=conv_w, conv_b=conv_b, conv_ln_g=conv_ln_g, conv_ln_b=conv_ln_b,
             rwkv_mu=rwkv_mu, rwkv_w0=rwkv_w0, rwkv_w2=rwkv_w2, rwkv_a0=rwkv_a0, rwkv_a2=rwkv_a2,
             rwkv_g2=rwkv_g2, rwkv_k_k=rwkv_k_k, rwkv_k_a=rwkv_k_a, rwkv_r_k=rwkv_r_k,
             rwkv_lnx_g=rwkv_lnx_g, rwkv_lnx_b=rwkv_lnx_b, w_out=w_out, ln1_g=ln1_g, ln1_b=ln1_b,
             ln2_g=ln2_g, ln2_b=ln2_b, ple_w=ple_w, ple_gate_w=ple_gate_w)
    depth = w_in.shape[0]
    alpha = (2 * depth) ** 0.25
    layer_w = [_layer_weights(l, a) for l in range(depth)]
    mixers_w = dict(ffn_g=ffn_w_gate.astype(BF16), ffn_u=ffn_w_up.astype(BF16), ffn_d=ffn_w_down.astype(BF16),
                    router_t=jnp.pad(moe_router.transpose(0, 2, 1), ((0, 0), (0, ROUTER_ROWS - N_EXPERTS), (0, 0))),
                    moe_g=moe_w_gate.astype(BF16), moe_u=moe_w_up.astype(BF16), moe_d=moe_w_down.astype(BF16))
    n_pages = page_table.shape[1]
    past_len = n_pages * PAGE_SIZE
    pos_prompt = jnp.arange(x_prompt.shape[1], dtype=jnp.int32)
    pos_sample = past_len + jnp.arange(x_sample.shape[1], dtype=jnp.int32)
    past = dict(cache_ckv=cache_ckv, cache_kr=cache_kr.transpose(0, 1, 3, 2), page_table=page_table,
                state_conv=state_conv, state_shift=state_shift, state_wkv=state_wkv)
    yp = _run_group(x_prompt, p_prompt, pos_prompt, a, layer_w, mixers_w, depth, alpha, None)
    ys = _run_group(x_sample, p_sample, pos_sample, a, layer_w, mixers_w, depth, alpha, past)
    return (yp[0], ys[0]) + yp[1:] + ys[1:]
```
